```python
import math
import jax
import jax.numpy as jnp
from jax import lax
import numpy as np

D_MODEL = 1024
BATCH = 8
SEQ = 2048
DEPTH = 4
DEC_BATCH = 128
DEC_SEQ = 8
PAST_LEN = 16384
PAGE_SIZE = 128

W_BR = D_MODEL // 2
N_BRANCH = 4
CONV_W = 3
S5_GROUP = 16
S5_GROUPS = W_BR // S5_GROUP
S5_STATE = 64
HG_HEADS = 4
HG_DK = W_BR // HG_HEADS
HG_DV = W_BR // HG_HEADS
GLA_HEADS = 4
GLA_DK = W_BR // (2 * GLA_HEADS)
GLA_DV = W_BR // GLA_HEADS
GLA_RANK = 16
GLA_GATE_NORM = 16.0
CHUNK = 64
EPS = 1e-6
SIZES = (W_BR, W_BR, W_BR, W_BR,
         W_BR, W_BR,
         HG_HEADS * HG_DK, HG_HEADS * HG_DK, HG_HEADS * HG_DV, W_BR,
         GLA_HEADS * GLA_DK, GLA_HEADS * GLA_DK, GLA_HEADS * GLA_DV, W_BR, GLA_RANK,
         N_BRANCH * D_MODEL)
D_IN = sum(SIZES)

kernel_name = 'parallel_gated_hybrid_conv_s5_hgrn2_gla_step'


def rmsnorm(x, g):
    xf = x.astype(jnp.float32)
    xf = xf * lax.rsqrt(jnp.mean(xf * xf, axis=-1, keepdims=True) + EPS)
    return xf * g.astype(jnp.float32)


def split_points():
    pts, s = [], 0
    for n in SIZES[:-1]:
        s += n
        pts.append(s)
    return pts


def short_conv(u, buf, w):
    L = u.shape[1]
    ext = jnp.concatenate([buf.astype(u.dtype), u], axis=1)
    y = ext[:, 0:L] * w[0]
    for j in range(1, CONV_W):
        y = y + ext[:, j:j + L] * w[j]
    return y, ext[:, L:]


def cmul(ar, ai, br, bi):
    return ar * br - ai * bi, ar * bi + ai * br


def s5_ssm(u, h0_re, h0_im, a_re, a_im, log_dt, b_re, b_im, c_re, c_im, d):
    f32 = jnp.float32
    bsz, L, _ = u.shape
    uf = u.astype(f32).reshape(bsz, L, S5_GROUPS, S5_GROUP)
    ar = a_re.astype(f32)
    ai = a_im.astype(f32)
    dt = jnp.exp(log_dt.astype(f32))[:, None]
    mag = jnp.exp(dt * ar)
    abar_r = mag * jnp.cos(dt * ai)
    abar_i = mag * jnp.sin(dt * ai)
    den = ar * ar + ai * ai
    zr = ((abar_r - 1.0) * ar + abar_i * ai) / den
    zi = (abar_i * ar - (abar_r - 1.0) * ai) / den
    bbar_r, bbar_i = cmul(zr[..., None], zi[..., None], b_re.astype(f32), b_im.astype(f32))
    bu_r = jnp.einsum('gnp,blgp->blgn', bbar_r, uf)
    bu_i = jnp.einsum('gnp,blgp->blgn', bbar_i, uf)
    i_r, i_i = cmul(abar_r, abar_i, h0_re.astype(f32), h0_im.astype(f32))
    bu_r = bu_r.at[:, 0].add(i_r)
    bu_i = bu_i.at[:, 0].add(i_i)
    a_r = jnp.broadcast_to(abar_r, (1, L, S5_GROUPS, S5_STATE))
    a_i = jnp.broadcast_to(abar_i, (1, L, S5_GROUPS, S5_STATE))

    def combine(e1, e2):
        a1r, a1i, b1r, b1i = e1
        a2r, a2i, b2r, b2i = e2
        nar, nai = cmul(a2r, a2i, a1r, a1i)
        nbr, nbi = cmul(a2r, a2i, b1r, b1i)
        return nar, nai, nbr + b2r, nbi + b2i

    _, _, h_r, h_i = lax.associative_scan(combine, (a_r, a_i, bu_r, bu_i), axis=1)
    y = (jnp.einsum('gpn,blgn->blgp', c_re.astype(f32), h_r)
         - jnp.einsum('gpn,blgn->blgp', c_im.astype(f32), h_i))
    y = y + d.astype(f32).reshape(S5_GROUPS, S5_GROUP) * uf
    return y.reshape(bsz, L, W_BR), h_r[:, -1], h_i[:, -1]


def gated_recurrence(q, k, v, logf, s0):
    f32 = jnp.float32
    bsz, L, H, K = q.shape
    V = v.shape[-1]
    c = math.gcd(L, CHUNK)
    n = L // c

    def to_chunks(t):
        return jnp.moveaxis(t.astype(f32).reshape(bsz, n, c, H, t.shape[-1]), 1, 0)

    mask = jnp.tril(jnp.ones((c, c), dtype=bool))[None, :, :, None, None]

    def step(S, inp):
        qc, kc, vc, gc = inp
        b = jnp.cumsum(gc, axis=1)
        o_inter = jnp.einsum('bthk,bhkv->bthv', qc * jnp.exp(b), S)
        diff = b[:, :, None] - b[:, None, :]
        decay = jnp.exp(jnp.where(mask, diff, -jnp.inf))
        att = jnp.einsum('bthk,bshk,btshk->bhts', qc, kc, decay)
        o_intra = jnp.einsum('bhts,bshv->bthv', att, vc)
        b_last = b[:, -1]
        k_dec = kc * jnp.exp(b_last[:, None] - b)
        S = jnp.exp(b_last)[..., None] * S + jnp.einsum('bshk,bshv->bhkv', k_dec, vc)
        return S, o_inter + o_intra

    S, o = lax.scan(step, s0.astype(f32), (to_chunks(q), to_chunks(k), to_chunks(v), to_chunks(logf)))
    o = jnp.moveaxis(o, 0, 1).reshape(bsz, L, H, V)
    return o, S


def trunk(x, st_conv, st_re, st_im, st_hg, st_gla, norm_w, w_in, conv_w, s5_a_re, s5_a_im,
          s5_log_dt, s5_b_re, s5_b_im, s5_c_re, s5_c_im, s5_d, w_glu, b_glu, hgrn_lb_raw,
          hgrn_norm, w_gk, b_gk, gla_norm, w_branch, w_out, final_norm):
    f32 = jnp.float32
    bsz, L, _ = x.shape
    dt = x.dtype
    lb_cum = jnp.cumsum(jax.nn.softmax(hgrn_lb_raw.astype(f32), axis=0), axis=0)
    lb_all = lb_cum - lb_cum[0:1]
    pts = split_points()
    new_conv, new_re, new_im, new_hg, new_gla = [], [], [], [], []
    for l in range(DEPTH):
        h = rmsnorm(x, norm_w[l]).astype(dt)
        proj = h @ w_in[l]
        (a_x, a_b, a_c, a_z, s_u, s_z, c_q, c_f, c_i, c_z,
         d_q, d_k, d_v, d_z, d_r, m_logit) = jnp.split(proj, pts, axis=-1)
        conv_out, cbuf = short_conv(a_c * a_x, st_conv[l], conv_w[l])
        y_a = a_b * conv_out * jax.nn.silu(a_z)
        y_s, h_r, h_i = s5_ssm(s_u, st_re[l], st_im[l], s5_a_re[l], s5_a_im[l], s5_log_dt[l],
                               s5_b_re[l], s5_b_im[l], s5_c_re[l], s5_c_im[l], s5_d[l])
        sg = jax.nn.gelu(y_s)
        glu = sg * jax.nn.sigmoid(sg @ w_glu[l].astype(f32) + b_glu[l].astype(f32))
        y_b = glu.astype(dt) * jax.nn.silu(s_z)
        lb = lb_all[l].reshape(HG_HEADS, HG_DK)
        fr = c_f.astype(f32).reshape(bsz, L, HG_HEADS, HG_DK)
        logf_c = jnp.logaddexp(jnp.log(lb), jnp.log1p(-lb) + jax.nn.log_sigmoid(fr))
        k_c = (1.0 - lb) * jax.nn.sigmoid(-fr)
        q_c = jax.nn.silu(c_q.astype(f32)).reshape(bsz, L, HG_HEADS, HG_DK) * (HG_DK ** -0.5)
        v_c = c_i.reshape(bsz, L, HG_HEADS, HG_DV)
        o_c, S_c = gated_recurrence(q_c, k_c, v_c, logf_c, st_hg[l])
        y_c = (rmsnorm(o_c, hgrn_norm[l]).reshape(bsz, L, W_BR) * jax.nn.silu(c_z.astype(f32))).astype(dt)
        gk = jax.nn.log_sigmoid((d_r @ w_gk[l] + b_gk[l]).astype(f32)) / GLA_GATE_NORM
        gk = gk.reshape(bsz, L, GLA_HEADS, GLA_DK)
        q_d = d_q.astype(f32).reshape(bsz, L, GLA_HEADS, GLA_DK) * (GLA_DK ** -0.5)
        k_d = d_k.reshape(bsz, L, GLA_HEADS, GLA_DK)
        v_d = d_v.reshape(bsz, L, GLA_HEADS, GLA_DV)
        o_d, S_d = gated_recurrence(q_d, k_d, v_d, gk, st_gla[l])
        y_d = (rmsnorm(o_d, gla_norm[l]).reshape(bsz, L, W_BR) * jax.nn.silu(d_z.astype(f32))).astype(dt)
        branches = jnp.stack([y_a, y_b, y_c, y_d], axis=2)
        p = jnp.einsum('blnw,nwd->blnd', branches, w_branch[l])
        gates = jax.nn.sigmoid(m_logit.reshape(bsz, L, N_BRANCH, D_MODEL))
        merged = jnp.sum(gates * p, axis=2)
        x = x + merged @ w_out[l]
        new_conv.append(cbuf.astype(dt))
        new_re.append(h_r.astype(dt))
        new_im.append(h_i.astype(dt))
        new_hg.append(S_c.astype(dt))
        new_gla.append(S_d.astype(dt))
    y = rmsnorm(x, final_norm).astype(dt)
    return (y, jnp.stack(new_conv, 0), jnp.stack(new_re, 0), jnp.stack(new_im, 0),
            jnp.stack(new_hg, 0), jnp.stack(new_gla, 0))


def setup_inputs(seed: int = 0) -> dict:
    key = jax.random.key(seed)
    ks = jax.random.split(key, 32)
    nrm = jax.random.normal
    f32 = jnp.float32
    n_idx = jnp.arange(S5_STATE, dtype=f32)
    return {
        'x_prompt': nrm(ks[0], (BATCH, SEQ, D_MODEL), f32),
        'x_sample': nrm(ks[1], (DEC_BATCH, DEC_SEQ, D_MODEL), f32),
        'state_conv': nrm(ks[2], (DEPTH, DEC_BATCH, CONV_W - 1, W_BR), f32),
        'state_ssm_re': 0.5 * nrm(ks[3], (DEPTH, DEC_BATCH, S5_GROUPS, S5_STATE), f32),
        'state_ssm_im': 0.5 * nrm(ks[4], (DEPTH, DEC_BATCH, S5_GROUPS, S5_STATE), f32),
        'state_hgrn': 0.5 * nrm(ks[5], (DEPTH, DEC_BATCH, HG_HEADS, HG_DK, HG_DV), f32),
        'state_gla': nrm(ks[6], (DEPTH, DEC_BATCH, GLA_HEADS, GLA_DK, GLA_DV), f32),
        'norm_w': 1.0 + 0.02 * nrm(ks[7], (DEPTH, D_MODEL), f32),
        'w_in': nrm(ks[8], (DEPTH, D_MODEL, D_IN), f32) * D_MODEL ** -0.5,
        'conv_w': nrm(ks[9], (DEPTH, CONV_W, W_BR), f32) * CONV_W ** -0.5,
        's5_a_re': -0.5 + 0.01 * nrm(ks[10], (DEPTH, S5_GROUPS, S5_STATE), f32),
        's5_a_im': math.pi * n_idx + 0.01 * nrm(ks[11], (DEPTH, S5_GROUPS, S5_STATE), f32),
        's5_log_dt': jax.random.uniform(ks[12], (DEPTH, S5_GROUPS), f32, math.log(1e-3), math.log(1e-1)),
        's5_b_re': nrm(ks[13], (DEPTH, S5_GROUPS, S5_STATE, S5_GROUP), f32) * (2 * S5_GROUP) ** -0.5,
        's5_b_im': nrm(ks[14], (DEPTH, S5_GROUPS, S5_STATE, S5_GROUP), f32) * (2 * S5_GROUP) ** -0.5,
        's5_c_re': nrm(ks[15], (DEPTH, S5_GROUPS, S5_GROUP, S5_STATE), f32) * (2 * S5_STATE) ** -0.5,
        's5_c_im': nrm(ks[16], (DEPTH, S5_GROUPS, S5_GROUP, S5_STATE), f32) * (2 * S5_STATE) ** -0.5,
        's5_d': nrm(ks[17], (DEPTH, W_BR), f32),
        'w_glu': nrm(ks[18], (DEPTH, W_BR, W_BR), f32) * W_BR ** -0.5,
        'b_glu': 0.01 * nrm(ks[19], (DEPTH, W_BR), f32),
        'hgrn_lb_raw': 0.1 * nrm(ks[20], (DEPTH, HG_HEADS * HG_DK), f32),
        'hgrn_norm': 1.0 + 0.02 * nrm(ks[21], (DEPTH, HG_DV), f32),
        'w_gk': nrm(ks[22], (DEPTH, GLA_RANK, GLA_HEADS * GLA_DK), f32) * GLA_RANK ** -0.5,
        'b_gk': 0.01 * nrm(ks[23], (DEPTH, GLA_HEADS * GLA_DK), f32),
        'gla_norm': 1.0 + 0.02 * nrm(ks[24], (DEPTH, GLA_DV), f32),
        'w_branch': nrm(ks[25], (DEPTH, N_BRANCH, W_BR, D_MODEL), f32) * W_BR ** -0.5,
        'w_out': nrm(ks[26], (DEPTH, D_MODEL, D_MODEL), f32) * D_MODEL ** -0.5,
        'final_norm': 1.0 + 0.02 * nrm(ks[27], (D_MODEL,), f32),
    }


def reference(x_prompt, x_sample, state_conv, state_ssm_re, state_ssm_im, state_hgrn, state_gla,
              norm_w, w_in, conv_w, s5_a_re, s5_a_im, s5_log_dt, s5_b_re, s5_b_im, s5_c_re, s5_c_im,
              s5_d, w_glu, b_glu, hgrn_lb_raw, hgrn_norm, w_gk, b_gk, gla_norm, w_branch, w_out,
              final_norm):
    dt = x_prompt.dtype
    bp = x_prompt.shape[0]
    z_conv = jnp.zeros((DEPTH, bp, CONV_W - 1, W_BR), dt)
    z_ssm = jnp.zeros((DEPTH, bp, S5_GROUPS, S5_STATE), dt)
    z_hg = jnp.zeros((DEPTH, bp, HG_HEADS, HG_DK, HG_DV), dt)
    z_gla = jnp.zeros((DEPTH, bp, GLA_HEADS, GLA_DK, GLA_DV), dt)
    y_prompt, conv_p, re_p, im_p, hg_p, gla_p = trunk(
        x_prompt, z_conv, z_ssm, z_ssm, z_hg, z_gla, norm_w, w_in, conv_w, s5_a_re, s5_a_im,
        s5_log_dt, s5_b_re, s5_b_im, s5_c_re, s5_c_im, s5_d, w_glu, b_glu, hgrn_lb_raw,
        hgrn_norm, w_gk, b_gk, gla_norm, w_branch, w_out, final_norm)
    y_sample, conv_s, re_s, im_s, hg_s, gla_s = trunk(
        x_sample, state_conv, state_ssm_re, state_ssm_im, state_hgrn, state_gla, norm_w, w_in,
        conv_w, s5_a_re, s5_a_im, s5_log_dt, s5_b_re, s5_b_im, s5_c_re, s5_c_im, s5_d, w_glu,
        b_glu, hgrn_lb_raw, hgrn_norm, w_gk, b_gk, gla_norm, w_branch, w_out, final_norm)
    return (y_prompt, y_sample, conv_p, conv_s, re_p, re_s, im_p, im_s, hg_p, hg_s, gla_p, gla_s)
```

```python
import functools
import math

import jax
import jax.numpy as jnp
from jax import lax
from jax.experimental import pallas as pl
from jax.experimental.pallas import tpu as pltpu

D_MODEL = 1024
DEPTH = 4
W_BR = 512
S5_GROUPS = 32
S5_GROUP = 16
S5_STATE = 64
S5_FLAT = S5_GROUPS * S5_STATE
HEADS = 4
HEAD_DIM = 128
GLA_DK = 64
GLA_RANK = 16
GLA_GATE_NORM = 16.0
EPS = 1e-6
N_GEN_HEADS = 2 * HEADS
DIAG = 8
SUBLANES = 8
LANES = 128
BF16_ROWS = 16
VMEM_LIMIT_BYTES = 56 * 1024 * 1024

_OFF_CQ = 3072
_OFF_DQ = 5120
_OFF_DK = 5376
_OFF_DV = 5632
_OFF_DR = 6656
_OFF_GATE = 6672

_A_AX, _A_AB, _A_AC, _A_AZ, _A_SU, _A_SZ, _A_GA, _A_GB = 0, 512, 1024, 1536, 2048, 2560, 3072, 4096
_A_COLS = 5120
_B_CQ, _B_CF, _B_CI, _B_CZ, _B_DQ, _B_DK, _B_DV, _B_DZ, _B_DR, _B_GC, _B_GD = (
    0, 512, 1024, 1536, 2048, 2560, 3072, 3584, 4096, 4224, 5248)
_B_COLS = 6272

_GELU_C = math.sqrt(2.0 / math.pi)
_BF = jnp.bfloat16
_F32 = jnp.float32


def _sigmoid(x):
    return 1.0 / (1.0 + jnp.exp(-x))


def _silu(x):
    return x * _sigmoid(x)


def _log_sigmoid(x):
    return jnp.minimum(x, 0.0) - jnp.log1p(jnp.exp(-jnp.abs(x)))


def _gelu_tanh(x):
    return 0.5 * x * (1.0 + jnp.tanh(_GELU_C * (x + 0.044715 * (x * x * x))))


def _rms_scale(x):
    return x * lax.rsqrt(jnp.mean(x * x, axis=-1, keepdims=True) + EPS)


def _dot(a, b):
    return jnp.dot(a, b, preferred_element_type=_F32)


def _mix_ab_kernel(x_ref, cs_ref, sr_ref, si_ref, nw_ref, w_ref, cw_ref, ar_ref, ai_ref, bdb_ref,
                   bdc_ref, d_ref, wglu_ref, bglu_ref, wbr_ref,
                   m_ref, cso_ref, sro_ref, sio_ref,
                   u_ref, bur_ref, bui_ref, y_ref, *, tb, tl):
    rows = tb * tl
    ngroup = tb // SUBLANES
    step = pl.program_id(1)

    @pl.when(step == 0)
    def _():
        cso_ref[...] = cs_ref[...]
        sro_ref[...] = sr_ref[...]
        sio_ref[...] = si_ref[...]

    x = x_ref[...].reshape(rows, D_MODEL)
    h = (_rms_scale(x) * nw_ref[...]).astype(_BF)

    def proj(c0, n):
        return _dot(h, w_ref[:, c0:c0 + n])

    v2 = proj(_A_AC, W_BR) * proj(_A_AX, W_BR)
    v3 = v2.reshape(tb, tl, W_BR)
    t_idx = lax.broadcasted_iota(jnp.int32, (tb, tl, W_BR), 1)
    buf = cso_ref[...]
    b0 = buf[:, 0:1, :]
    b1 = buf[:, 1:2, :]
    r1 = pltpu.roll(v2, 1, axis=0).reshape(tb, tl, W_BR)
    r2 = pltpu.roll(v2, 2, axis=0).reshape(tb, tl, W_BR)
    p1 = jnp.where(t_idx == 0, b1, r1)
    p2 = jnp.where(t_idx == 0, b0, jnp.where(t_idx == 1, b1, r2))
    cw = cw_ref[...]
    conv = p2 * cw[0:1, :] + p1 * cw[1:2, :] + v3 * cw[2:3, :]
    cso_ref[...] = v3[:, tl - 2:tl, :]
    y_a = proj(_A_AB, W_BR) * conv.reshape(rows, W_BR) * _silu(proj(_A_AZ, W_BR))

    u = proj(_A_SU, W_BR)
    nslab = W_BR // LANES
    for c in range(nslab):
        u_ref[c] = u[:, c * LANES:(c + 1) * LANES]
    up = jnp.concatenate(
        [jnp.concatenate(
            [u_ref[c, pl.ds(g * SUBLANES * tl + t, SUBLANES, stride=tl), :]
             for g in range(ngroup) for t in range(tl)], axis=0)
         for c in range(nslab)], axis=1).astype(_BF)
    for hh in range(2):
        uh = up[:, 256 * hh:256 * hh + 256]
        bur_ref[:, 1024 * hh:1024 * hh + 1024] = _dot(uh, bdb_ref[hh, 0])
        bui_ref[:, 1024 * hh:1024 * hh + 1024] = _dot(uh, bdb_ref[hh, 1])

    lane_chunk = 1024
    for g in range(ngroup):
        base = g * SUBLANES * tl
        srow = slice(g * SUBLANES, (g + 1) * SUBLANES)
        for lc in range(S5_FLAT // lane_chunk):
            ls = slice(lc * lane_chunk, (lc + 1) * lane_chunk)
            a_r = jnp.broadcast_to(ar_ref[:, ls], (SUBLANES, lane_chunk))
            a_i = jnp.broadcast_to(ai_ref[:, ls], (SUBLANES, lane_chunk))

            def body(t, carry, base=base, ls=ls, a_r=a_r, a_i=a_i):
                hr, hi = carry
                row = pl.multiple_of(base + t * SUBLANES, SUBLANES)
                nr = a_r * hr - a_i * hi + bur_ref[pl.ds(row, SUBLANES), ls]
                ni = a_r * hi + a_i * hr + bui_ref[pl.ds(row, SUBLANES), ls]
                bur_ref[pl.ds(row, SUBLANES), ls] = nr
                bui_ref[pl.ds(row, SUBLANES), ls] = ni
                return nr, ni

            hr, hi = lax.fori_loop(0, tl, body, (sro_ref[srow, ls], sio_ref[srow, ls]))
            sro_ref[srow, ls] = hr
            sio_ref[srow, ls] = hi

    for hh in range(2):
        cs = slice(1024 * hh, 1024 * hh + 1024)
        y_h = (_dot(bur_ref[:, cs].astype(_BF), bdc_ref[hh, 0])
               + _dot(bui_ref[:, cs].astype(_BF), bdc_ref[hh, 1]))
        y_ref[2 * hh] = y_h[:, 0:LANES]
        y_ref[2 * hh + 1] = y_h[:, LANES:2 * LANES]
    ys = jnp.concatenate(
        [jnp.concatenate(
            [y_ref[c, pl.ds(g * SUBLANES * tl + b8, tl, stride=SUBLANES), :]
             for g in range(ngroup) for b8 in range(SUBLANES)], axis=0)
         for c in range(nslab)], axis=1)
    y_s = ys + d_ref[...] * u
    sg = _gelu_tanh(y_s)
    glu = sg * _sigmoid(_dot(sg.astype(_BF), wglu_ref[...]) + bglu_ref[...])
    y_b = glu * _silu(proj(_A_SZ, W_BR))

    merged = (_sigmoid(proj(_A_GA, D_MODEL)) * _dot(y_a.astype(_BF), wbr_ref[0])
              + _sigmoid(proj(_A_GB, D_MODEL)) * _dot(y_b.astype(_BF), wbr_ref[1]))
    m_ref[...] = merged.reshape(tb, tl, D_MODEL)


def _mix_ab(x, st_conv, st_re, st_im, p, layer, *, tb, tl):
    bsz, seq, _ = x.shape
    rows = tb * tl
    grid = (bsz // tb, seq // tl)

    def lw(shape):
        nd = len(shape)
        return pl.BlockSpec((None,) + shape, lambda i, j, nd=nd: (layer,) + (0,) * nd)

    in_specs = [
        pl.BlockSpec((tb, tl, D_MODEL), lambda i, j: (i, j, 0)),
        pl.BlockSpec((None, tb, 2, W_BR), lambda i, j: (layer, i, 0, 0)),
        pl.BlockSpec((None, tb, S5_FLAT), lambda i, j: (layer, i, 0)),
        pl.BlockSpec((None, tb, S5_FLAT), lambda i, j: (layer, i, 0)),
        lw((1, D_MODEL)), lw((D_MODEL, _A_COLS)), lw((3, W_BR)), lw((1, S5_FLAT)), lw((1, S5_FLAT)),
        lw((2, 2, 256, 1024)), lw((2, 2, 1024, 256)), lw((1, W_BR)), lw((W_BR, W_BR)),
        lw((1, W_BR)), lw((2, W_BR, D_MODEL)),
    ]
    out_specs = [
        pl.BlockSpec((tb, tl, D_MODEL), lambda i, j: (i, j, 0)),
        pl.BlockSpec((tb, 2, W_BR), lambda i, j: (i, 0, 0)),
        pl.BlockSpec((tb, S5_FLAT), lambda i, j: (i, 0)),
        pl.BlockSpec((tb, S5_FLAT), lambda i, j: (i, 0)),
    ]
    out_shape = [
        jax.ShapeDtypeStruct((bsz, seq, D_MODEL), _F32),
        jax.ShapeDtypeStruct((bsz, 2, W_BR), _F32),
        jax.ShapeDtypeStruct((bsz, S5_FLAT), _F32),
        jax.ShapeDtypeStruct((bsz, S5_FLAT), _F32),
    ]
    scratch = [
        pltpu.VMEM((W_BR // LANES, rows, LANES), _F32),
        pltpu.VMEM((rows, S5_FLAT), _F32),
        pltpu.VMEM((rows, S5_FLAT), _F32),
        pltpu.VMEM((W_BR // LANES, rows, LANES), _F32),
    ]
    return pl.pallas_call(
        functools.partial(_mix_ab_kernel, tb=tb, tl=tl),
        grid=grid, in_specs=in_specs, out_specs=out_specs, out_shape=out_shape,
        scratch_shapes=scratch, name="mix_ab",
        compiler_params=pltpu.CompilerParams(
            dimension_semantics=("arbitrary", "arbitrary"), vmem_limit_bytes=VMEM_LIMIT_BYTES),
    )(x, st_conv, st_re, st_im, p["norm_w"], p["w_a"], p["conv_w"], p["a_r"], p["a_i"], p["bdb"],
      p["bdc"], p["s5_d"], p["w_glu"], p["b_glu"], p["w_br_ab"])


def _mix_cd_kernel(x_ref, mab_ref, hg_ref, gl_ref, nw_ref, w_ref, loglb_ref, log1m_ref, onem_ref,
                   hgn_ref, wgk_ref, bgk_ref, gmask_ref, gln_ref, wbr_ref, wout_ref, fn_ref,
                   *rest, tb, tl, chunk, seg, last):
    if last:
        xo_ref, yo_ref, hgo_ref, glo_ref = rest[:4]
        scr = rest[4:]
    else:
        xo_ref, hgo_ref, glo_ref = rest[:3]
        yo_ref = None
        scr = rest[3:]
    q_ref, k_ref, lf_ref, v_ref, o_ref, qi_ref, kd_ref, bb_ref, oi_ref = scr
    rows = tb * tl
    nchunk = rows // chunk
    nseg = chunk // seg
    step = pl.program_id(1)

    @pl.when(step == 0)
    def _():
        hgo_ref[...] = hg_ref[...]
        glo_ref[...] = gl_ref[...]

    x = x_ref[...].reshape(rows, D_MODEL)
    h = (_rms_scale(x) * nw_ref[...]).astype(_BF)

    def proj(c0, n):
        return _dot(h, w_ref[:, c0:c0 + n])

    cf = proj(_B_CF, W_BR)
    gate_b = log1m_ref[...] + _log_sigmoid(cf)
    gate_a = loglb_ref[...]
    lf_ref[:, 0:W_BR] = (jnp.maximum(gate_a, gate_b)
                         + jnp.log1p(jnp.exp(-jnp.abs(gate_a - gate_b))))
    k_ref[:, 0:W_BR] = onem_ref[...] * _sigmoid(-cf)
    q_ref[:, 0:W_BR] = _silu(proj(_B_CQ, W_BR)) * (HEAD_DIM ** -0.5)
    v_ref[:, 0:W_BR] = proj(_B_CI, W_BR)
    gk_lin = _dot(proj(_B_DR, LANES).astype(_BF), wgk_ref[...]) + bgk_ref[...]
    lf_ref[:, W_BR:2 * W_BR] = _log_sigmoid(gk_lin) * (1.0 / GLA_GATE_NORM) * gmask_ref[...]
    q_ref[:, W_BR:2 * W_BR] = proj(_B_DQ, W_BR) * (GLA_DK ** -0.5)
    k_ref[:, W_BR:2 * W_BR] = proj(_B_DK, W_BR)
    v_ref[:, W_BR:2 * W_BR] = proj(_B_DV, W_BR)

    row_i = lax.broadcasted_iota(jnp.int32, (chunk, HEAD_DIM), 0)
    row_seg = row_i & (seg - 1)
    row_diag = row_i & (DIAG - 1)
    ri = lax.broadcasted_iota(jnp.int32, (chunk, chunk), 0)
    ci = lax.broadcasted_iota(jnp.int32, (chunk, chunk), 1)
    eye = (lax.broadcasted_iota(jnp.int32, (HEAD_DIM, HEAD_DIM), 0)
           == lax.broadcasted_iota(jnp.int32, (HEAD_DIM, HEAD_DIM), 1))
    levels = []
    m = DIAG
    while 2 * m <= seg:
        levels.append(m)
        m *= 2

    def load_state(s, hd):
        if hd < HEADS:
            return hgo_ref[s, hd]
        return jnp.concatenate([glo_ref[s, hd - HEADS], jnp.zeros((GLA_DK, HEAD_DIM), _F32)], axis=0)

    def store_state(s, hd, val):
        if hd < HEADS:
            hgo_ref[s, hd] = val
        else:
            glo_ref[s, hd - HEADS] = val[0:GLA_DK, :]

    def chunk_body(c):
        r0 = c * chunk if isinstance(c, int) else pl.multiple_of(c * chunk, chunk)
        for hd in range(N_GEN_HEADS):
            sl = slice(hd * HEAD_DIM, (hd + 1) * HEAD_DIM)
            q = q_ref[pl.ds(r0, chunk), sl]
            k = k_ref[pl.ds(r0, chunk), sl]
            v = v_ref[pl.ds(r0, chunk), sl]
            b = lf_ref[pl.ds(r0, chunk), sl]
            d = 1
            while d < seg:
                b = b + jnp.where(row_seg >= d, pltpu.roll(b, d, axis=0), 0.0)
                d *= 2
            att = jnp.zeros((chunk, chunk), _F32)
            for m in levels:
                b3 = b.reshape(chunk // (2 * m), 2 * m, HEAD_DIM)
                ref = b3[:, m - 1:m, :]
                eq = jnp.minimum(b3 - ref, 0.0).reshape(chunk, HEAD_DIM)
                ek = jnp.minimum(ref - b3, 0.0).reshape(chunk, HEAD_DIM)
                a_m = lax.dot_general((q * jnp.exp(eq)).astype(_BF), (k * jnp.exp(ek)).astype(_BF),
                                      (((1,), (1,)), ((), ())), preferred_element_type=_F32)
                sh = (2 * m).bit_length() - 1
                mask = (((ri >> sh) == (ci >> sh)) & ((ri & (2 * m - 1)) >= m)
                        & ((ci & (2 * m - 1)) < m))
                att = jnp.where(mask, a_m, att)
            for dd in range(DIAG):
                if dd == 0:
                    w_d = jnp.sum(q * k, axis=-1, keepdims=True)
                else:
                    e = jnp.where(row_diag >= dd, b - pltpu.roll(b, dd, axis=0), 0.0)
                    w_d = jnp.sum(q * pltpu.roll(k, dd, axis=0) * jnp.exp(e), axis=-1, keepdims=True)
                att = jnp.where((ri - ci == dd) & ((ri & (DIAG - 1)) >= dd), w_d, att)
            oi_ref[...] = _dot(att.astype(_BF), v.astype(_BF))
            b_last = b.reshape(nseg, seg, HEAD_DIM)[:, seg - 1:seg, :]
            qi_ref[...] = q * jnp.exp(b)
            kd_ref[...] = k * jnp.exp(
                (b_last - b.reshape(nseg, seg, HEAD_DIM)).reshape(chunk, HEAD_DIM))
            bb_ref[...] = b

            def seg_body(s, hd=hd, sl=sl, r0=r0):
                if isinstance(s, int):
                    s0 = s * seg
                else:
                    s0 = pl.multiple_of(s * seg, seg)
                sidx = s if nchunk == 1 else 0
                state = load_state(sidx, hd)
                q_s = qi_ref[pl.ds(s0, seg), :]
                kd_s = kd_ref[pl.ds(s0, seg), :]
                v_s = v_ref[pl.ds(r0 + s0, seg), sl]
                if seg < BF16_ROWS:
                    zpad = jnp.zeros((BF16_ROWS - seg, HEAD_DIM), _F32)
                    q_s = jnp.concatenate([q_s, zpad], axis=0)
                    kd_s = jnp.concatenate([kd_s, zpad], axis=0)
                    v_s = jnp.concatenate([v_s, zpad], axis=0)
                q_s = q_s.astype(_BF)
                kd_s = kd_s.astype(_BF)
                v_s = v_s.astype(_BF)
                o_inter = _dot(q_s, state.astype(_BF))[0:seg, :]
                o_ref[pl.ds(r0 + s0, seg), sl] = oi_ref[pl.ds(s0, seg), :] + o_inter
                bl = bb_ref[pl.ds(s0 + seg - 1, 1), :]
                dec_col = jnp.sum(jnp.where(eye, jnp.exp(bl), 0.0), axis=1, keepdims=True)
                upd = lax.dot_general(kd_s, v_s, (((0,), (0,)), ((), ())),
                                      preferred_element_type=_F32)
                store_state(sidx, hd, state * dec_col + upd)

            if nseg == 1:
                seg_body(0)
            else:
                def seg_loop(s, carry):
                    seg_body(s)
                    return carry
                lax.fori_loop(0, nseg, seg_loop, 0)

    if nchunk == 1:
        chunk_body(0)
    else:
        def chunk_loop(c, carry):
            chunk_body(c)
            return carry
        lax.fori_loop(0, nchunk, chunk_loop, 0)

    def normed(lo, norm_row):
        parts = []
        for hd in range(lo, lo + HEADS):
            o_h = o_ref[:, hd * HEAD_DIM:(hd + 1) * HEAD_DIM]
            parts.append(_rms_scale(o_h) * norm_row)
        return jnp.concatenate(parts, axis=1)

    y_c = normed(0, hgn_ref[...]) * _silu(proj(_B_CZ, W_BR))
    y_d = normed(HEADS, gln_ref[...]) * _silu(proj(_B_DZ, W_BR))
    merged = (mab_ref[...].reshape(rows, D_MODEL)
              + _sigmoid(proj(_B_GC, D_MODEL)) * _dot(y_c.astype(_BF), wbr_ref[0])
              + _sigmoid(proj(_B_GD, D_MODEL)) * _dot(y_d.astype(_BF), wbr_ref[1]))
    x_new = x + _dot(merged.astype(_BF), wout_ref[...])
    xo_ref[...] = x_new.reshape(tb, tl, D_MODEL)
    if last:
        yo_ref[...] = (_rms_scale(x_new) * fn_ref[...]).reshape(tb, tl, D_MODEL)


def _mix_cd(x, mab, st_hg, st_gla, p, layer, *, tb, tl, chunk, seg, last):
    bsz, seq, _ = x.shape
    rows = tb * tl
    grid = (bsz // tb, seq // tl)

    def lw(shape):
        nd = len(shape)
        return pl.BlockSpec((None,) + shape, lambda i, j, nd=nd: (layer,) + (0,) * nd)

    act = pl.BlockSpec((tb, tl, D_MODEL), lambda i, j: (i, j, 0))
    hg_spec = pl.BlockSpec((tb, HEADS, HEAD_DIM, HEAD_DIM), lambda i, j: (i, 0, 0, 0))
    gl_spec = pl.BlockSpec((tb, HEADS, GLA_DK, HEAD_DIM), lambda i, j: (i, 0, 0, 0))
    in_specs = [
        act, act,
        pl.BlockSpec((None, tb, HEADS, HEAD_DIM, HEAD_DIM), lambda i, j: (layer, i, 0, 0, 0)),
        pl.BlockSpec((None, tb, HEADS, GLA_DK, HEAD_DIM), lambda i, j: (layer, i, 0, 0, 0)),
        lw((1, D_MODEL)), lw((D_MODEL, _B_COLS)), lw((1, W_BR)), lw((1, W_BR)), lw((1, W_BR)),
        lw((1, HEAD_DIM)), lw((LANES, W_BR)), lw((1, W_BR)),
        pl.BlockSpec((1, W_BR), lambda i, j: (0, 0)),
        lw((1, HEAD_DIM)), lw((2, W_BR, D_MODEL)), lw((D_MODEL, D_MODEL)),
        pl.BlockSpec((1, D_MODEL), lambda i, j: (0, 0)),
    ]
    act_shape = jax.ShapeDtypeStruct((bsz, seq, D_MODEL), _F32)
    out_specs = [act] + ([act] if last else []) + [hg_spec, gl_spec]
    out_shape = [act_shape] + ([act_shape] if last else []) + [
        jax.ShapeDtypeStruct(st_hg.shape[1:], _F32), jax.ShapeDtypeStruct(st_gla.shape[1:], _F32)]
    wide = N_GEN_HEADS * HEAD_DIM
    scratch = [pltpu.VMEM((rows, wide), _F32) for _ in range(5)] + [
        pltpu.VMEM((chunk, HEAD_DIM), _F32) for _ in range(4)]
    return pl.pallas_call(
        functools.partial(_mix_cd_kernel, tb=tb, tl=tl, chunk=chunk, seg=seg, last=last),
        grid=grid, in_specs=in_specs, out_specs=out_specs, out_shape=out_shape,
        scratch_shapes=scratch, name="mix_cd",
        compiler_params=pltpu.CompilerParams(
            dimension_semantics=("arbitrary", "arbitrary"), vmem_limit_bytes=VMEM_LIMIT_BYTES),
    )(x, mab, st_hg, st_gla, p["norm_w"], p["w_b"], p["log_lb"], p["log1m_lb"], p["onem_lb"],
      p["hgrn_norm"], p["w_gk"], p["b_gk"], p["gk_mask"], p["gla_norm"], p["w_br_cd"], p["w_out"],
      p["final_norm"])


def _pad_gla_heads(w):
    lead = w.shape[:-1]
    w4 = w.reshape(lead + (HEADS, GLA_DK))
    w4 = jnp.pad(w4, [(0, 0)] * len(lead) + [(0, 0), (0, HEAD_DIM - GLA_DK)])
    return w4.reshape(lead + (HEADS * HEAD_DIM,))


def _prep_params(norm_w, w_in, conv_w, s5_a_re, s5_a_im, s5_log_dt, s5_b_re, s5_b_im, s5_c_re,
                 s5_c_im, s5_d, w_glu, b_glu, hgrn_lb_raw, hgrn_norm, w_gk, b_gk, gla_norm,
                 w_branch, w_out, final_norm):
    f32 = _F32
    p = {}
    p["norm_w"] = norm_w.astype(f32)[:, None, :]
    p["w_a"] = jnp.concatenate(
        [w_in[:, :, 0:_OFF_CQ], w_in[:, :, _OFF_GATE:_OFF_GATE + 2 * D_MODEL]], axis=2).astype(_BF)
    p["w_b"] = jnp.concatenate(
        [w_in[:, :, _OFF_CQ:_OFF_DQ],
         _pad_gla_heads(w_in[:, :, _OFF_DQ:_OFF_DK]),
         _pad_gla_heads(w_in[:, :, _OFF_DK:_OFF_DV]),
         w_in[:, :, _OFF_DV:_OFF_DR],
         jnp.pad(w_in[:, :, _OFF_DR:_OFF_GATE], ((0, 0), (0, 0), (0, LANES - GLA_RANK))),
         w_in[:, :, _OFF_GATE + 2 * D_MODEL:_OFF_GATE + 4 * D_MODEL]], axis=2).astype(_BF)
    p["conv_w"] = conv_w.astype(f32)

    ar = s5_a_re.astype(f32)
    ai = s5_a_im.astype(f32)
    dt = jnp.exp(s5_log_dt.astype(f32))[..., None]
    mag = jnp.exp(dt * ar)
    abar_r = mag * jnp.cos(dt * ai)
    abar_i = mag * jnp.sin(dt * ai)
    den = ar * ar + ai * ai
    zr = ((abar_r - 1.0) * ar + abar_i * ai) / den
    zi = (abar_i * ar - (abar_r - 1.0) * ai) / den
    b_re = s5_b_re.astype(f32)
    b_im = s5_b_im.astype(f32)
    bbar_r = zr[..., None] * b_re - zi[..., None] * b_im
    bbar_i = zr[..., None] * b_im + zi[..., None] * b_re
    eye = jnp.eye(S5_GROUP, dtype=f32)

    def bd_in(bb):
        x = bb.reshape(DEPTH, 2, S5_GROUP, S5_STATE, S5_GROUP)
        x = jnp.transpose(x, (0, 1, 2, 4, 3))[:, :, :, :, None, :]
        x = x * eye[None, None, :, None, :, None]
        return x.reshape(DEPTH, 2, 256, 1024)

    def bd_out(c):
        x = c.reshape(DEPTH, 2, S5_GROUP, S5_GROUP, S5_STATE)
        x = jnp.transpose(x, (0, 1, 2, 4, 3))[:, :, :, :, None, :]
        x = x * eye[None, None, :, None, :, None]
        return x.reshape(DEPTH, 2, 1024, 256)

    p["bdb"] = jnp.stack([bd_in(bbar_r), bd_in(bbar_i)], axis=2).astype(_BF)
    p["bdc"] = jnp.stack([bd_out(s5_c_re.astype(f32)), bd_out(-s5_c_im.astype(f32))],
                         axis=2).astype(_BF)
    p["a_r"] = abar_r.reshape(DEPTH, 1, S5_FLAT)
    p["a_i"] = abar_i.reshape(DEPTH, 1, S5_FLAT)
    p["s5_d"] = s5_d.astype(f32)[:, None, :]
    p["w_glu"] = w_glu.astype(_BF)
    p["b_glu"] = b_glu.astype(f32)[:, None, :]
    p["w_br_ab"] = w_branch[:, 0:2].astype(_BF)
    p["w_br_cd"] = w_branch[:, 2:4].astype(_BF)
    p["w_out"] = w_out.astype(_BF)

    lb_cum = jnp.cumsum(jax.nn.softmax(hgrn_lb_raw.astype(f32), axis=0), axis=0)
    lb = lb_cum - lb_cum[0:1]
    p["log_lb"] = jnp.log(lb)[:, None, :]
    p["log1m_lb"] = jnp.log1p(-lb)[:, None, :]
    p["onem_lb"] = (1.0 - lb)[:, None, :]
    p["hgrn_norm"] = hgrn_norm.astype(f32)[:, None, :]
    p["gla_norm"] = gla_norm.astype(f32)[:, None, :]
    p["w_gk"] = jnp.pad(_pad_gla_heads(w_gk), ((0, 0), (0, LANES - GLA_RANK), (0, 0))).astype(_BF)
    p["b_gk"] = _pad_gla_heads(b_gk.astype(f32))[:, None, :]
    p["gk_mask"] = _pad_gla_heads(jnp.ones((1, HEADS * GLA_DK), f32))
    p["final_norm"] = final_norm.astype(f32)[None, :]
    return p


def _trunk(x, st_conv, st_re, st_im, st_hg, st_gla, p, *, ab_cfg, cd_cfg):
    bsz = x.shape[0]
    new_conv, new_re, new_im, new_hg, new_gla = [], [], [], [], []
    y = None
    st_re = st_re.reshape(DEPTH, bsz, S5_FLAT)
    st_im = st_im.reshape(DEPTH, bsz, S5_FLAT)
    for layer in range(DEPTH):
        mab, cbuf, h_r, h_i = _mix_ab(x, st_conv, st_re, st_im, p, layer, **ab_cfg)
        last = layer == DEPTH - 1
        outs = _mix_cd(x, mab, st_hg, st_gla, p, layer, last=last, **cd_cfg)
        if last:
            x, y, s_c, s_d = outs
        else:
            x, s_c, s_d = outs
        new_conv.append(cbuf)
        new_re.append(h_r.reshape(bsz, S5_GROUPS, S5_STATE))
        new_im.append(h_i.reshape(bsz, S5_GROUPS, S5_STATE))
        new_hg.append(s_c)
        new_gla.append(s_d)
    return (y, jnp.stack(new_conv, 0), jnp.stack(new_re, 0), jnp.stack(new_im, 0),
            jnp.stack(new_hg, 0), jnp.stack(new_gla, 0))


_PROMPT_AB = dict(tb=8, tl=32)
_PROMPT_CD = dict(tb=1, tl=256, chunk=64, seg=64)
_SAMPLE_AB = dict(tb=32, tl=8)
_SAMPLE_CD = dict(tb=8, tl=8, chunk=64, seg=8)


@jax.jit
def _forward(x_prompt, x_sample, state_conv, state_ssm_re, state_ssm_im, state_hgrn, state_gla,
             *params):
    p = _prep_params(*params)
    bp = x_prompt.shape[0]
    dt = x_prompt.dtype
    z_conv = jnp.zeros((DEPTH, bp, 2, W_BR), dt)
    z_ssm = jnp.zeros((DEPTH, bp, S5_GROUPS, S5_STATE), dt)
    z_hg = jnp.zeros((DEPTH, bp, HEADS, HEAD_DIM, HEAD_DIM), dt)
    z_gla = jnp.zeros((DEPTH, bp, HEADS, GLA_DK, HEAD_DIM), dt)
    y_p, conv_p, re_p, im_p, hg_p, gla_p = _trunk(
        x_prompt, z_conv, z_ssm, z_ssm, z_hg, z_gla, p, ab_cfg=_PROMPT_AB, cd_cfg=_PROMPT_CD)
    y_s, conv_s, re_s, im_s, hg_s, gla_s = _trunk(
        x_sample, state_conv, state_ssm_re, state_ssm_im, state_hgrn, state_gla, p,
        ab_cfg=_SAMPLE_AB, cd_cfg=_SAMPLE_CD)
    return (y_p, y_s, conv_p, conv_s, re_p, re_s, im_p, im_s, hg_p, hg_s, gla_p, gla_s)


def kernel(x_prompt, x_sample, state_conv, state_ssm_re, state_ssm_im, state_hgrn, state_gla,
           norm_w, w_in, conv_w, s5_a_re, s5_a_im, s5_log_dt, s5_b_re, s5_b_im, s5_c_re, s5_c_im,
           s5_d, w_glu, b_glu, hgrn_lb_raw, hgrn_norm, w_gk, b_gk, gla_norm, w_branch, w_out,
           final_norm):
    return _forward(x_prompt, x_sample, state_conv, state_ssm_re, state_ssm_im, state_hgrn,
                    state_gla, norm_w, w_in, conv_w, s5_a_re, s5_a_im, s5_log_dt, s5_b_re, s5_b_im,
                    s5_c_re, s5_c_im, s5_d, w_glu, b_glu, hgrn_lb_raw, hgrn_norm, w_gk, b_gk,
                    gla_norm, w_branch, w_out, final_norm)
```

```python
import functools
import math

import jax
import jax.numpy as jnp
from jax import lax
from jax.experimental import pallas as pl
from jax.experimental.pallas import tpu as pltpu

D_MODEL = 1024
DEPTH = 4
W_BR = 512
S5_GROUPS = 32
S5_GROUP = 16
S5_STATE = 64
S5_FLAT = S5_GROUPS * S5_STATE
HEADS = 4
HEAD_DIM = 128
GLA_DK = 64
GLA_RANK = 16
GLA_GATE_NORM = 16.0
EPS = 1e-6
N_GEN_HEADS = 2 * HEADS
DIAG = 4
SUBLANES = 8
LANES = 128
BF16_ROWS = 16
VMEM_LIMIT_BYTES = 56 * 1024 * 1024

_OFF_CQ = 3072
_OFF_DQ = 5120
_OFF_DK = 5376
_OFF_DV = 5632
_OFF_DR = 6656
_OFF_GATE = 6672

_A_AX, _A_AB, _A_AC, _A_AZ, _A_SU, _A_SZ, _A_GA, _A_GB = 0, 512, 1024, 1536, 2048, 2560, 3072, 4096
_A_COLS = 5120
_B_CQ, _B_CF, _B_CI, _B_CZ, _B_DQ, _B_DK, _B_DV, _B_DZ, _B_DR, _B_GC, _B_GD = (
    0, 512, 1024, 1536, 2048, 2560, 3072, 3584, 4096, 4224, 5248)
_B_COLS = 6272

_GELU_C = math.sqrt(2.0 / math.pi)
_BF = jnp.bfloat16
_F32 = jnp.float32


def _sigmoid(x):
    return 1.0 / (1.0 + jnp.exp(-x))


def _silu(x):
    return x * _sigmoid(x)


def _log_sigmoid(x):
    return jnp.minimum(x, 0.0) - jnp.log1p(jnp.exp(-jnp.abs(x)))


def _gelu_tanh(x):
    return 0.5 * x * (1.0 + jnp.tanh(_GELU_C * (x + 0.044715 * (x * x * x))))


def _rms_scale(x):
    return x * lax.rsqrt(jnp.mean(x * x, axis=-1, keepdims=True) + EPS)


def _dot(a, b):
    return jnp.dot(a, b, preferred_element_type=_F32)


def _mix_ab_kernel(x_ref, cs_ref, sr_ref, si_ref, nw_ref, w_ref, cw_ref, ar_ref, ai_ref, bdb_ref,
                   bdc_ref, d_ref, wglu_ref, bglu_ref, wbr_ref,
                   m_ref, cso_ref, sro_ref, sio_ref,
                   u_ref, bur_ref, bui_ref, y_ref, *, tb, tl):
    rows = tb * tl
    ngroup = tb // SUBLANES
    step = pl.program_id(1)

    @pl.when(step == 0)
    def _():
        cso_ref[...] = cs_ref[...]
        sro_ref[...] = sr_ref[...]
        sio_ref[...] = si_ref[...]

    x = x_ref[...].reshape(rows, D_MODEL)
    h = (_rms_scale(x) * nw_ref[...]).astype(_BF)

    def proj(c0, n):
        return _dot(h, w_ref[:, c0:c0 + n])

    v2 = proj(_A_AC, W_BR) * proj(_A_AX, W_BR)
    v3 = v2.reshape(tb, tl, W_BR)
    t_idx = lax.broadcasted_iota(jnp.int32, (tb, tl, W_BR), 1)
    buf = cso_ref[...]
    b0 = buf[:, 0:1, :]
    b1 = buf[:, 1:2, :]
    r1 = pltpu.roll(v2, 1, axis=0).reshape(tb, tl, W_BR)
    r2 = pltpu.roll(v2, 2, axis=0).reshape(tb, tl, W_BR)
    p1 = jnp.where(t_idx == 0, b1, r1)
    p2 = jnp.where(t_idx == 0, b0, jnp.where(t_idx == 1, b1, r2))
    cw = cw_ref[...]
    conv = p2 * cw[0:1, :] + p1 * cw[1:2, :] + v3 * cw[2:3, :]
    cso_ref[...] = v3[:, tl - 2:tl, :]
    y_a = proj(_A_AB, W_BR) * conv.reshape(rows, W_BR) * _silu(proj(_A_AZ, W_BR))

    u = proj(_A_SU, W_BR)
    nslab = W_BR // LANES
    for c in range(nslab):
        u_ref[c] = u[:, c * LANES:(c + 1) * LANES]
    up = jnp.concatenate(
        [jnp.concatenate(
            [u_ref[c, pl.ds(g * SUBLANES * tl + t, SUBLANES, stride=tl), :]
             for g in range(ngroup) for t in range(tl)], axis=0)
         for c in range(nslab)], axis=1).astype(_BF)
    for hh in range(2):
        uh = up[:, 256 * hh:256 * hh + 256]
        bur_ref[:, 1024 * hh:1024 * hh + 1024] = _dot(uh, bdb_ref[hh, 0])
        bui_ref[:, 1024 * hh:1024 * hh + 1024] = _dot(uh, bdb_ref[hh, 1])

    lane_chunk = 1024
    for g in range(ngroup):
        base = g * SUBLANES * tl
        srow = slice(g * SUBLANES, (g + 1) * SUBLANES)
        for lc in range(S5_FLAT // lane_chunk):
            ls = slice(lc * lane_chunk, (lc + 1) * lane_chunk)
            a_r = jnp.broadcast_to(ar_ref[:, ls], (SUBLANES, lane_chunk))
            a_i = jnp.broadcast_to(ai_ref[:, ls], (SUBLANES, lane_chunk))

            def body(t, carry, base=base, ls=ls, a_r=a_r, a_i=a_i):
                hr, hi = carry
                row = pl.multiple_of(base + t * SUBLANES, SUBLANES)
                nr = a_r * hr - a_i * hi + bur_ref[pl.ds(row, SUBLANES), ls]
                ni = a_r * hi + a_i * hr + bui_ref[pl.ds(row, SUBLANES), ls]
                bur_ref[pl.ds(row, SUBLANES), ls] = nr
                bui_ref[pl.ds(row, SUBLANES), ls] = ni
                return nr, ni

            hr, hi = lax.fori_loop(0, tl, body, (sro_ref[srow, ls], sio_ref[srow, ls]))
            sro_ref[srow, ls] = hr
            sio_ref[srow, ls] = hi

    for hh in range(2):
        cs = slice(1024 * hh, 1024 * hh + 1024)
        y_h = (_dot(bur_ref[:, cs].astype(_BF), bdc_ref[hh, 0])
               + _dot(bui_ref[:, cs].astype(_BF), bdc_ref[hh, 1]))
        y_ref[2 * hh] = y_h[:, 0:LANES]
        y_ref[2 * hh + 1] = y_h[:, LANES:2 * LANES]
    ys = jnp.concatenate(
        [jnp.concatenate(
            [y_ref[c, pl.ds(g * SUBLANES * tl + b8, tl, stride=SUBLANES), :]
             for g in range(ngroup) for b8 in range(SUBLANES)], axis=0)
         for c in range(nslab)], axis=1)
    y_s = ys + d_ref[...] * u
    sg = _gelu_tanh(y_s)
    glu = sg * _sigmoid(_dot(sg.astype(_BF), wglu_ref[...]) + bglu_ref[...])
    y_b = glu * _silu(proj(_A_SZ, W_BR))

    merged = (_sigmoid(proj(_A_GA, D_MODEL)) * _dot(y_a.astype(_BF), wbr_ref[0])
              + _sigmoid(proj(_A_GB, D_MODEL)) * _dot(y_b.astype(_BF), wbr_ref[1]))
    m_ref[...] = merged.reshape(tb, tl, D_MODEL)


def _mix_ab(x, st_conv, st_re, st_im, p, layer, *, tb, tl):
    bsz, seq, _ = x.shape
    rows = tb * tl
    grid = (bsz // tb, seq // tl)

    def lw(shape):
        nd = len(shape)
        return pl.BlockSpec((None,) + shape, lambda i, j, nd=nd: (layer,) + (0,) * nd,
                            pipeline_mode=pl.Buffered(1))

    in_specs = [
        pl.BlockSpec((tb, tl, D_MODEL), lambda i, j: (i, j, 0)),
        pl.BlockSpec((None, tb, 2, W_BR), lambda i, j: (layer, i, 0, 0)),
        pl.BlockSpec((None, tb, S5_FLAT), lambda i, j: (layer, i, 0)),
        pl.BlockSpec((None, tb, S5_FLAT), lambda i, j: (layer, i, 0)),
        lw((1, D_MODEL)), lw((D_MODEL, _A_COLS)), lw((3, W_BR)), lw((1, S5_FLAT)), lw((1, S5_FLAT)),
        lw((2, 2, 256, 1024)), lw((2, 2, 1024, 256)), lw((1, W_BR)), lw((W_BR, W_BR)),
        lw((1, W_BR)), lw((2, W_BR, D_MODEL)),
    ]
    out_specs = [
        pl.BlockSpec((tb, tl, D_MODEL), lambda i, j: (i, j, 0)),
        pl.BlockSpec((tb, 2, W_BR), lambda i, j: (i, 0, 0)),
        pl.BlockSpec((tb, S5_FLAT), lambda i, j: (i, 0)),
        pl.BlockSpec((tb, S5_FLAT), lambda i, j: (i, 0)),
    ]
    out_shape = [
        jax.ShapeDtypeStruct((bsz, seq, D_MODEL), _F32),
        jax.ShapeDtypeStruct((bsz, 2, W_BR), _F32),
        jax.ShapeDtypeStruct((bsz, S5_FLAT), _F32),
        jax.ShapeDtypeStruct((bsz, S5_FLAT), _F32),
    ]
    scratch = [
        pltpu.VMEM((W_BR // LANES, rows, LANES), _F32),
        pltpu.VMEM((rows, S5_FLAT), _F32),
        pltpu.VMEM((rows, S5_FLAT), _F32),
        pltpu.VMEM((W_BR // LANES, rows, LANES), _F32),
    ]
    return pl.pallas_call(
        functools.partial(_mix_ab_kernel, tb=tb, tl=tl),
        grid=grid, in_specs=in_specs, out_specs=out_specs, out_shape=out_shape,
        scratch_shapes=scratch, name="mix_ab",
        compiler_params=pltpu.CompilerParams(
            dimension_semantics=("arbitrary", "arbitrary"), vmem_limit_bytes=VMEM_LIMIT_BYTES),
    )(x, st_conv, st_re, st_im, p["norm_w"], p["w_a"], p["conv_w"], p["a_r"], p["a_i"], p["bdb"],
      p["bdc"], p["s5_d"], p["w_glu"], p["b_glu"], p["w_br_ab"])


def _mix_cd_kernel(x_ref, mab_ref, hg_ref, gl_ref, nw_ref, w_ref, loglb_ref, log1m_ref, onem_ref,
                   hgn_ref, wgk_ref, bgk_ref, gmask_ref, gln_ref, wbr_ref, wout_ref, fn_ref,
                   *rest, tb, tl, chunk, seg, last):
    if last:
        xo_ref, yo_ref, hgo_ref, glo_ref = rest[:4]
        scr = rest[4:]
    else:
        xo_ref, hgo_ref, glo_ref = rest[:3]
        yo_ref = None
        scr = rest[3:]
    q_ref, k_ref, lf_ref, v_ref, o_ref = scr
    rows = tb * tl
    nchunk = rows // chunk
    nseg = chunk // seg
    step = pl.program_id(1)

    @pl.when(step == 0)
    def _():
        hgo_ref[...] = hg_ref[...]
        glo_ref[...] = gl_ref[...]

    x = x_ref[...].reshape(rows, D_MODEL)
    h = (_rms_scale(x) * nw_ref[...]).astype(_BF)

    def proj(c0, n):
        return _dot(h, w_ref[:, c0:c0 + n])

    cf = proj(_B_CF, W_BR)
    gate_b = log1m_ref[...] + _log_sigmoid(cf)
    gate_a = loglb_ref[...]
    lf_ref[:, 0:W_BR] = (jnp.maximum(gate_a, gate_b)
                         + jnp.log1p(jnp.exp(-jnp.abs(gate_a - gate_b))))
    k_ref[:, 0:W_BR] = onem_ref[...] * _sigmoid(-cf)
    q_ref[:, 0:W_BR] = _silu(proj(_B_CQ, W_BR)) * (HEAD_DIM ** -0.5)
    v_ref[:, 0:W_BR] = proj(_B_CI, W_BR)
    gk_lin = _dot(proj(_B_DR, LANES).astype(_BF), wgk_ref[...]) + bgk_ref[...]
    lf_ref[:, W_BR:2 * W_BR] = _log_sigmoid(gk_lin) * (1.0 / GLA_GATE_NORM) * gmask_ref[...]
    q_ref[:, W_BR:2 * W_BR] = proj(_B_DQ, W_BR) * (GLA_DK ** -0.5)
    k_ref[:, W_BR:2 * W_BR] = proj(_B_DK, W_BR)
    v_ref[:, W_BR:2 * W_BR] = proj(_B_DV, W_BR)

    nblk = chunk // SUBLANES
    blk_per_seg = seg // SUBLANES
    sub_i = lax.broadcasted_iota(jnp.int32, (SUBLANES, HEAD_DIM), 0)
    row_diag = lax.broadcasted_iota(jnp.int32, (chunk, HEAD_DIM), 0) & (DIAG - 1)
    ri = lax.broadcasted_iota(jnp.int32, (chunk, chunk), 0)
    ci = lax.broadcasted_iota(jnp.int32, (chunk, chunk), 1)
    eye = (lax.broadcasted_iota(jnp.int32, (HEAD_DIM, HEAD_DIM), 0)
           == lax.broadcasted_iota(jnp.int32, (HEAD_DIM, HEAD_DIM), 1))
    levels = []
    m = DIAG
    while 2 * m <= seg:
        levels.append(m)
        m *= 2
    level_masks = []
    for m in levels:
        sh = (2 * m).bit_length() - 1
        level_masks.append(((ri >> sh) == (ci >> sh)) & ((ri & (2 * m - 1)) >= m)
                           & ((ci & (2 * m - 1)) < m))
    diag_masks = [(ri - ci == dd) & ((ri & (DIAG - 1)) >= dd) for dd in range(DIAG)]

    def block_roll(a, shift):
        return jnp.concatenate(
            [pltpu.roll(a[j * SUBLANES:(j + 1) * SUBLANES, :], shift, axis=0)
             for j in range(nblk)], axis=0)

    def seg_cumsum(lf):
        out = []
        carry = None
        for j in range(nblk):
            blk = lf[j * SUBLANES:(j + 1) * SUBLANES, :]
            d = 1
            while d < SUBLANES:
                blk = blk + jnp.where(sub_i >= d, pltpu.roll(blk, d, axis=0), 0.0)
                d *= 2
            if j % blk_per_seg != 0:
                blk = blk + carry
            carry = blk[SUBLANES - 1:SUBLANES, :]
            out.append(blk)
        return jnp.concatenate(out, axis=0)

    def load_state(s, hd):
        if hd < HEADS:
            return hgo_ref[s, hd]
        return jnp.concatenate([glo_ref[s, hd - HEADS], jnp.zeros((GLA_DK, HEAD_DIM), _F32)], axis=0)

    def store_state(s, hd, val):
        if hd < HEADS:
            hgo_ref[s, hd] = val
        else:
            glo_ref[s, hd - HEADS] = val[0:GLA_DK, :]

    def chunk_body(c):
        r0 = c * chunk if isinstance(c, int) else pl.multiple_of(c * chunk, chunk)
        for hd in range(N_GEN_HEADS):
            sl = slice(hd * HEAD_DIM, (hd + 1) * HEAD_DIM)
            q = q_ref[pl.ds(r0, chunk), sl]
            k = k_ref[pl.ds(r0, chunk), sl]
            v = v_ref[pl.ds(r0, chunk), sl]
            b = seg_cumsum(lf_ref[pl.ds(r0, chunk), sl])
            att = jnp.zeros((chunk, chunk), _F32)
            for m, mask in zip(levels, level_masks):
                b3 = b.reshape(chunk // (2 * m), 2 * m, HEAD_DIM)
                dec = jnp.exp(-jnp.abs(b3 - b3[:, m - 1:m, :])).reshape(chunk, HEAD_DIM)
                a_m = lax.dot_general((q * dec).astype(_BF), (k * dec).astype(_BF),
                                      (((1,), (1,)), ((), ())), preferred_element_type=_F32)
                att = jnp.where(mask, a_m, att)
            for dd in range(DIAG):
                if dd == 0:
                    w_d = jnp.sum(q * k, axis=-1, keepdims=True)
                else:
                    e = jnp.where(row_diag >= dd, b - block_roll(b, dd), 0.0)
                    w_d = jnp.sum(q * block_roll(k, dd) * jnp.exp(e), axis=-1, keepdims=True)
                att = jnp.where(diag_masks[dd], w_d, att)
            o_intra = _dot(att.astype(_BF), v.astype(_BF))
            b_last = b.reshape(nseg, seg, HEAD_DIM)[:, seg - 1:seg, :]
            q_in = q * jnp.exp(b)
            k_dec = k * jnp.exp(
                (b_last - b.reshape(nseg, seg, HEAD_DIM)).reshape(chunk, HEAD_DIM))

            for s in range(nseg):
                rs = slice(s * seg, (s + 1) * seg)
                sidx = s if nchunk == 1 else 0
                state = load_state(sidx, hd)
                q_s, kd_s, v_s = q_in[rs], k_dec[rs], v[rs]
                if seg < BF16_ROWS:
                    zpad = jnp.zeros((BF16_ROWS - seg, HEAD_DIM), _F32)
                    q_s = jnp.concatenate([q_s, zpad], axis=0)
                    kd_s = jnp.concatenate([kd_s, zpad], axis=0)
                    v_s = jnp.concatenate([v_s, zpad], axis=0)
                o_inter = _dot(q_s.astype(_BF), state.astype(_BF))[0:seg, :]
                o_ref[pl.ds(r0 + s * seg, seg), sl] = o_intra[rs] + o_inter
                bl = b[(s + 1) * seg - 1:(s + 1) * seg, :]
                dec_col = jnp.sum(jnp.where(eye, jnp.exp(bl), 0.0), axis=1, keepdims=True)
                upd = lax.dot_general(kd_s.astype(_BF), v_s.astype(_BF), (((0,), (0,)), ((), ())),
                                      preferred_element_type=_F32)
                store_state(sidx, hd, state * dec_col + upd)

    if nchunk == 1:
        chunk_body(0)
    else:
        def chunk_loop(c, carry):
            chunk_body(c)
            return carry
        lax.fori_loop(0, nchunk, chunk_loop, 0)

    def normed(lo, norm_row):
        parts = []
        for hd in range(lo, lo + HEADS):
            o_h = o_ref[:, hd * HEAD_DIM:(hd + 1) * HEAD_DIM]
            parts.append(_rms_scale(o_h) * norm_row)
        return jnp.concatenate(parts, axis=1)

    y_c = normed(0, hgn_ref[...]) * _silu(proj(_B_CZ, W_BR))
    y_d = normed(HEADS, gln_ref[...]) * _silu(proj(_B_DZ, W_BR))
    merged = (mab_ref[...].reshape(rows, D_MODEL)
              + _sigmoid(proj(_B_GC, D_MODEL)) * _dot(y_c.astype(_BF), wbr_ref[0])
              + _sigmoid(proj(_B_GD, D_MODEL)) * _dot(y_d.astype(_BF), wbr_ref[1]))
    x_new = x + _dot(merged.astype(_BF), wout_ref[...])
    xo_ref[...] = x_new.reshape(tb, tl, D_MODEL)
    if last:
        yo_ref[...] = (_rms_scale(x_new) * fn_ref[...]).reshape(tb, tl, D_MODEL)


def _mix_cd(x, mab, st_hg, st_gla, p, layer, *, tb, tl, chunk, seg, last):
    bsz, seq, _ = x.shape
    rows = tb * tl
    grid = (bsz // tb, seq // tl)

    def lw(shape):
        nd = len(shape)
        return pl.BlockSpec((None,) + shape, lambda i, j, nd=nd: (layer,) + (0,) * nd,
                            pipeline_mode=pl.Buffered(1))

    act = pl.BlockSpec((tb, tl, D_MODEL), lambda i, j: (i, j, 0))
    hg_spec = pl.BlockSpec((tb, HEADS, HEAD_DIM, HEAD_DIM), lambda i, j: (i, 0, 0, 0))
    gl_spec = pl.BlockSpec((tb, HEADS, GLA_DK, HEAD_DIM), lambda i, j: (i, 0, 0, 0))
    in_specs = [
        act, act,
        pl.BlockSpec((None, tb, HEADS, HEAD_DIM, HEAD_DIM), lambda i, j: (layer, i, 0, 0, 0)),
        pl.BlockSpec((None, tb, HEADS, GLA_DK, HEAD_DIM), lambda i, j: (layer, i, 0, 0, 0)),
        lw((1, D_MODEL)), lw((D_MODEL, _B_COLS)), lw((1, W_BR)), lw((1, W_BR)), lw((1, W_BR)),
        lw((1, HEAD_DIM)), lw((LANES, W_BR)), lw((1, W_BR)),
        pl.BlockSpec((1, W_BR), lambda i, j: (0, 0)),
        lw((1, HEAD_DIM)), lw((2, W_BR, D_MODEL)), lw((D_MODEL, D_MODEL)),
        pl.BlockSpec((1, D_MODEL), lambda i, j: (0, 0)),
    ]
    act_shape = jax.ShapeDtypeStruct((bsz, seq, D_MODEL), _F32)
    out_specs = [act] + ([act] if last else []) + [hg_spec, gl_spec]
    out_shape = [act_shape] + ([act_shape] if last else []) + [
        jax.ShapeDtypeStruct(st_hg.shape[1:], _F32), jax.ShapeDtypeStruct(st_gla.shape[1:], _F32)]
    wide = N_GEN_HEADS * HEAD_DIM
    scratch = [pltpu.VMEM((rows, wide), _F32) for _ in range(5)]
    return pl.pallas_call(
        functools.partial(_mix_cd_kernel, tb=tb, tl=tl, chunk=chunk, seg=seg, last=last),
        grid=grid, in_specs=in_specs, out_specs=out_specs, out_shape=out_shape,
        scratch_shapes=scratch, name="mix_cd",
        compiler_params=pltpu.CompilerParams(
            dimension_semantics=("arbitrary", "arbitrary"), vmem_limit_bytes=VMEM_LIMIT_BYTES),
    )(x, mab, st_hg, st_gla, p["norm_w"], p["w_b"], p["log_lb"], p["log1m_lb"], p["onem_lb"],
      p["hgrn_norm"], p["w_gk"], p["b_gk"], p["gk_mask"], p["gla_norm"], p["w_br_cd"], p["w_out"],
      p["final_norm"])


def _pad_gla_heads(w):
    lead = w.shape[:-1]
    w4 = w.reshape(lead + (HEADS, GLA_DK))
    w4 = jnp.pad(w4, [(0, 0)] * len(lead) + [(0, 0), (0, HEAD_DIM - GLA_DK)])
    return w4.reshape(lead + (HEADS * HEAD_DIM,))


def _prep_params(norm_w, w_in, conv_w, s5_a_re, s5_a_im, s5_log_dt, s5_b_re, s5_b_im, s5_c_re,
                 s5_c_im, s5_d, w_glu, b_glu, hgrn_lb_raw, hgrn_norm, w_gk, b_gk, gla_norm,
                 w_branch, w_out, final_norm):
    f32 = _F32
    p = {}
    p["norm_w"] = norm_w.astype(f32)[:, None, :]
    p["w_a"] = jnp.concatenate(
        [w_in[:, :, 0:_OFF_CQ], w_in[:, :, _OFF_GATE:_OFF_GATE + 2 * D_MODEL]], axis=2).astype(_BF)
    p["w_b"] = jnp.concatenate(
        [w_in[:, :, _OFF_CQ:_OFF_DQ],
         _pad_gla_heads(w_in[:, :, _OFF_DQ:_OFF_DK]),
         _pad_gla_heads(w_in[:, :, _OFF_DK:_OFF_DV]),
         w_in[:, :, _OFF_DV:_OFF_DR],
         jnp.pad(w_in[:, :, _OFF_DR:_OFF_GATE], ((0, 0), (0, 0), (0, LANES - GLA_RANK))),
         w_in[:, :, _OFF_GATE + 2 * D_MODEL:_OFF_GATE + 4 * D_MODEL]], axis=2).astype(_BF)
    p["conv_w"] = conv_w.astype(f32)

    ar = s5_a_re.astype(f32)
    ai = s5_a_im.astype(f32)
    dt = jnp.exp(s5_log_dt.astype(f32))[..., None]
    mag = jnp.exp(dt * ar)
    abar_r = mag * jnp.cos(dt * ai)
    abar_i = mag * jnp.sin(dt * ai)
    den = ar * ar + ai * ai
    zr = ((abar_r - 1.0) * ar + abar_i * ai) / den
    zi = (abar_i * ar - (abar_r - 1.0) * ai) / den
    b_re = s5_b_re.astype(f32)
    b_im = s5_b_im.astype(f32)
    bbar_r = zr[..., None] * b_re - zi[..., None] * b_im
    bbar_i = zr[..., None] * b_im + zi[..., None] * b_re
    eye = jnp.eye(S5_GROUP, dtype=f32)

    def bd_in(bb):
        x = bb.reshape(DEPTH, 2, S5_GROUP, S5_STATE, S5_GROUP)
        x = jnp.transpose(x, (0, 1, 2, 4, 3))[:, :, :, :, None, :]
        x = x * eye[None, None, :, None, :, None]
        return x.reshape(DEPTH, 2, 256, 1024)

    def bd_out(c):
        x = c.reshape(DEPTH, 2, S5_GROUP, S5_GROUP, S5_STATE)
        x = jnp.transpose(x, (0, 1, 2, 4, 3))[:, :, :, :, None, :]
        x = x * eye[None, None, :, None, :, None]
        return x.reshape(DEPTH, 2, 1024, 256)

    p["bdb"] = jnp.stack([bd_in(bbar_r), bd_in(bbar_i)], axis=2).astype(_BF)
    p["bdc"] = jnp.stack([bd_out(s5_c_re.astype(f32)), bd_out(-s5_c_im.astype(f32))],
                         axis=2).astype(_BF)
    p["a_r"] = abar_r.reshape(DEPTH, 1, S5_FLAT)
    p["a_i"] = abar_i.reshape(DEPTH, 1, S5_FLAT)
    p["s5_d"] = s5_d.astype(f32)[:, None, :]
    p["w_glu"] = w_glu.astype(_BF)
    p["b_glu"] = b_glu.astype(f32)[:, None, :]
    p["w_br_ab"] = w_branch[:, 0:2].astype(_BF)
    p["w_br_cd"] = w_branch[:, 2:4].astype(_BF)
    p["w_out"] = w_out.astype(_BF)

    lb_cum = jnp.cumsum(jax.nn.softmax(hgrn_lb_raw.astype(f32), axis=0), axis=0)
    lb = lb_cum - lb_cum[0:1]
    p["log_lb"] = jnp.log(lb)[:, None, :]
    p["log1m_lb"] = jnp.log1p(-lb)[:, None, :]
    p["onem_lb"] = (1.0 - lb)[:, None, :]
    p["hgrn_norm"] = hgrn_norm.astype(f32)[:, None, :]
    p["gla_norm"] = gla_norm.astype(f32)[:, None, :]
    p["w_gk"] = jnp.pad(_pad_gla_heads(w_gk), ((0, 0), (0, LANES - GLA_RANK), (0, 0))).astype(_BF)
    p["b_gk"] = _pad_gla_heads(b_gk.astype(f32))[:, None, :]
    p["gk_mask"] = _pad_gla_heads(jnp.ones((1, HEADS * GLA_DK), f32))
    p["final_norm"] = final_norm.astype(f32)[None, :]
    return p


def _trunk(x, st_conv, st_re, st_im, st_hg, st_gla, p, *, ab_cfg, cd_cfg):
    bsz = x.shape[0]
    new_conv, new_re, new_im, new_hg, new_gla = [], [], [], [], []
    y = None
    st_re = st_re.reshape(DEPTH, bsz, S5_FLAT)
    st_im = st_im.reshape(DEPTH, bsz, S5_FLAT)
    for layer in range(DEPTH):
        mab, cbuf, h_r, h_i = _mix_ab(x, st_conv, st_re, st_im, p, layer, **ab_cfg)
        last = layer == DEPTH - 1
        outs = _mix_cd(x, mab, st_hg, st_gla, p, layer, last=last, **cd_cfg)
        if last:
            x, y, s_c, s_d = outs
        else:
            x, s_c, s_d = outs
        new_conv.append(cbuf)
        new_re.append(h_r.reshape(bsz, S5_GROUPS, S5_STATE))
        new_im.append(h_i.reshape(bsz, S5_GROUPS, S5_STATE))
        new_hg.append(s_c)
        new_gla.append(s_d)
    return (y, jnp.stack(new_conv, 0), jnp.stack(new_re, 0), jnp.stack(new_im, 0),
            jnp.stack(new_hg, 0), jnp.stack(new_gla, 0))


_PROMPT_AB = dict(tb=8, tl=32)
_PROMPT_CD = dict(tb=1, tl=256, chunk=64, seg=64)
_SAMPLE_AB = dict(tb=32, tl=8)
_SAMPLE_CD = dict(tb=16, tl=8, chunk=128, seg=8)


@jax.jit
def _forward(x_prompt, x_sample, state_conv, state_ssm_re, state_ssm_im, state_hgrn, state_gla,
             *params):
    p = _prep_params(*params)
    bp = x_prompt.shape[0]
    dt = x_prompt.dtype
    z_conv = jnp.zeros((DEPTH, bp, 2, W_BR), dt)
    z_ssm = jnp.zeros((DEPTH, bp, S5_GROUPS, S5_STATE), dt)
    z_hg = jnp.zeros((DEPTH, bp, HEADS, HEAD_DIM, HEAD_DIM), dt)
    z_gla = jnp.zeros((DEPTH, bp, HEADS, GLA_DK, HEAD_DIM), dt)
    y_p, conv_p, re_p, im_p, hg_p, gla_p = _trunk(
        x_prompt, z_conv, z_ssm, z_ssm, z_hg, z_gla, p, ab_cfg=_PROMPT_AB, cd_cfg=_PROMPT_CD)
    y_s, conv_s, re_s, im_s, hg_s, gla_s = _trunk(
        x_sample, state_conv, state_ssm_re, state_ssm_im, state_hgrn, state_gla, p,
        ab_cfg=_SAMPLE_AB, cd_cfg=_SAMPLE_CD)
    return (y_p, y_s, conv_p, conv_s, re_p, re_s, im_p, im_s, hg_p, hg_s, gla_p, gla_s)


def kernel(x_prompt, x_sample, state_conv, state_ssm_re, state_ssm_im, state_hgrn, state_gla,
           norm_w, w_in, conv_w, s5_a_re, s5_a_im, s5_log_dt, s5_b_re, s5_b_im, s5_c_re, s5_c_im,
           s5_d, w_glu, b_glu, hgrn_lb_raw, hgrn_norm, w_gk, b_gk, gla_norm, w_branch, w_out,
           final_norm):
    return _forward(x_prompt, x_sample, state_conv, state_ssm_re, state_ssm_im, state_hgrn,
                    state_gla, norm_w, w_in, conv_w, s5_a_re, s5_a_im, s5_log_dt, s5_b_re, s5_b_im,
                    s5_c_re, s5_c_im, s5_d, w_glu, b_glu, hgrn_lb_raw, hgrn_norm, w_gk, b_gk,
                    gla_norm, w_branch, w_out, final_norm)
```

```python
import functools
import math

import jax
import jax.numpy as jnp
from jax import lax
from jax.experimental import pallas as pl
from jax.experimental.pallas import tpu as pltpu

D_MODEL = 1024
DEPTH = 4
W_BR = 512
S5_GROUPS = 32
S5_GROUP = 16
S5_STATE = 64
S5_FLAT = S5_GROUPS * S5_STATE
HEADS = 4
HEAD_DIM = 128
GLA_DK = 64
GLA_RANK = 16
GLA_GATE_NORM = 16.0
EPS = 1e-6
N_GEN_HEADS = 2 * HEADS
DIAG = 8
SUBLANES = 8
LANES = 128
BF16_ROWS = 16
VMEM_LIMIT_BYTES = 56 * 1024 * 1024

_OFF_CQ = 3072
_OFF_DQ = 5120
_OFF_DK = 5376
_OFF_DV = 5632
_OFF_DR = 6656
_OFF_GATE = 6672

_A_AX, _A_AB, _A_AC, _A_AZ, _A_SU, _A_SZ, _A_GA, _A_GB = 0, 512, 1024, 1536, 2048, 2560, 3072, 4096
_A_COLS = 5120
_B_CQ, _B_CF, _B_CI, _B_CZ, _B_DQ, _B_DK, _B_DV, _B_DZ, _B_DR, _B_GC, _B_GD = (
    0, 512, 1024, 1536, 2048, 2560, 3072, 3584, 4096, 4224, 5248)
_B_COLS = 6272
_G_CZ, _G_DZ, _G_GC, _G_GD = 0, 512, 1024, 2048
_G_COLS = 3072
_G_SLAB = 256

_GELU_C = math.sqrt(2.0 / math.pi)
_LOG2E = math.log2(math.e)
_BF = jnp.bfloat16
_F32 = jnp.float32


def _sigmoid(x):
    return 1.0 / (1.0 + jnp.exp(-x))


def _silu(x):
    return x * _sigmoid(x)


def _softplus_neg_abs(x):
    return jnp.log(1.0 + jnp.exp(-jnp.abs(x)))


def _log_sigmoid(x):
    return jnp.minimum(x, 0.0) - _softplus_neg_abs(x)


def _gelu_tanh(x):
    return 0.5 * x * (1.0 + jnp.tanh(_GELU_C * (x + 0.044715 * (x * x * x))))


def _rms_scale(x):
    return x * lax.rsqrt(jnp.mean(x * x, axis=-1, keepdims=True) + EPS)


def _dot(a, b):
    return jnp.dot(a, b, preferred_element_type=_F32)


def _mix_ab_kernel(x_ref, cs_ref, sr_ref, si_ref, nw_ref, w_ref, cw_ref, ar_ref, ai_ref, bdb_ref,
                   bdc_ref, d_ref, wglu_ref, bglu_ref, wbr_ref,
                   m_ref, cso_ref, sro_ref, sio_ref,
                   u_ref, bur_ref, bui_ref, y_ref, *, tb, tl):
    rows = tb * tl
    ngroup = tb // SUBLANES
    step = pl.program_id(1)

    @pl.when(step == 0)
    def _():
        cso_ref[...] = cs_ref[...]
        sro_ref[...] = sr_ref[...]
        sio_ref[...] = si_ref[...]

    x = x_ref[...].reshape(rows, D_MODEL)
    h = (_rms_scale(x) * nw_ref[...]).astype(_BF)

    def proj(c0, n):
        return _dot(h, w_ref[:, c0:c0 + n])

    v2 = proj(_A_AC, W_BR) * proj(_A_AX, W_BR)
    v3 = v2.reshape(tb, tl, W_BR)
    t_idx = lax.broadcasted_iota(jnp.int32, (tb, tl, W_BR), 1)
    buf = cso_ref[...]
    b0 = buf[:, 0:1, :]
    b1 = buf[:, 1:2, :]
    r1 = pltpu.roll(v2, 1, axis=0).reshape(tb, tl, W_BR)
    r2 = pltpu.roll(v2, 2, axis=0).reshape(tb, tl, W_BR)
    p1 = jnp.where(t_idx == 0, b1, r1)
    p2 = jnp.where(t_idx == 0, b0, jnp.where(t_idx == 1, b1, r2))
    cw = cw_ref[...]
    conv = p2 * cw[0:1, :] + p1 * cw[1:2, :] + v3 * cw[2:3, :]
    cso_ref[...] = v3[:, tl - 2:tl, :]
    y_a = proj(_A_AB, W_BR) * conv.reshape(rows, W_BR) * _silu(proj(_A_AZ, W_BR))

    u = proj(_A_SU, W_BR)
    nslab = W_BR // LANES
    for c in range(nslab):
        u_ref[c] = u[:, c * LANES:(c + 1) * LANES]
    up = jnp.concatenate(
        [jnp.concatenate(
            [u_ref[c, pl.ds(g * SUBLANES * tl + t, SUBLANES, stride=tl), :]
             for g in range(ngroup) for t in range(tl)], axis=0)
         for c in range(nslab)], axis=1).astype(_BF)
    for hh in range(2):
        uh = up[:, 256 * hh:256 * hh + 256]
        bur_ref[:, 1024 * hh:1024 * hh + 1024] = _dot(uh, bdb_ref[hh, 0])
        bui_ref[:, 1024 * hh:1024 * hh + 1024] = _dot(uh, bdb_ref[hh, 1])

    lane_chunk = 1024
    for g in range(ngroup):
        base = g * SUBLANES * tl
        srow = slice(g * SUBLANES, (g + 1) * SUBLANES)
        for lc in range(S5_FLAT // lane_chunk):
            ls = slice(lc * lane_chunk, (lc + 1) * lane_chunk)
            a_r = jnp.broadcast_to(ar_ref[:, ls], (SUBLANES, lane_chunk))
            a_i = jnp.broadcast_to(ai_ref[:, ls], (SUBLANES, lane_chunk))

            def body(t, carry, base=base, ls=ls, a_r=a_r, a_i=a_i):
                hr, hi = carry
                row = pl.multiple_of(base + t * SUBLANES, SUBLANES)
                nr = a_r * hr - a_i * hi + bur_ref[pl.ds(row, SUBLANES), ls]
                ni = a_r * hi + a_i * hr + bui_ref[pl.ds(row, SUBLANES), ls]
                bur_ref[pl.ds(row, SUBLANES), ls] = nr
                bui_ref[pl.ds(row, SUBLANES), ls] = ni
                return nr, ni

            hr, hi = lax.fori_loop(0, tl, body, (sro_ref[srow, ls], sio_ref[srow, ls]))
            sro_ref[srow, ls] = hr
            sio_ref[srow, ls] = hi

    for hh in range(2):
        cs = slice(1024 * hh, 1024 * hh + 1024)
        y_h = (_dot(bur_ref[:, cs].astype(_BF), bdc_ref[hh, 0])
               + _dot(bui_ref[:, cs].astype(_BF), bdc_ref[hh, 1]))
        y_ref[2 * hh] = y_h[:, 0:LANES]
        y_ref[2 * hh + 1] = y_h[:, LANES:2 * LANES]
    ys = jnp.concatenate(
        [jnp.concatenate(
            [y_ref[c, pl.ds(g * SUBLANES * tl + b8, tl, stride=SUBLANES), :]
             for g in range(ngroup) for b8 in range(SUBLANES)], axis=0)
         for c in range(nslab)], axis=1)
    y_s = ys + d_ref[...] * u
    sg = _gelu_tanh(y_s)
    glu = sg * _sigmoid(_dot(sg.astype(_BF), wglu_ref[...]) + bglu_ref[...])
    y_b = glu * _silu(proj(_A_SZ, W_BR))

    merged = (_sigmoid(proj(_A_GA, D_MODEL)) * _dot(y_a.astype(_BF), wbr_ref[0])
              + _sigmoid(proj(_A_GB, D_MODEL)) * _dot(y_b.astype(_BF), wbr_ref[1]))
    m_ref[...] = merged.reshape(tb, tl, D_MODEL)


def _mix_ab(x, st_conv, st_re, st_im, p, layer, *, tb, tl):
    bsz, seq, _ = x.shape
    rows = tb * tl
    grid = (bsz // tb, seq // tl)

    def lw(shape):
        nd = len(shape)
        return pl.BlockSpec((None,) + shape, lambda i, j, nd=nd: (layer,) + (0,) * nd,
                            pipeline_mode=pl.Buffered(1))

    in_specs = [
        pl.BlockSpec((tb, tl, D_MODEL), lambda i, j: (i, j, 0)),
        pl.BlockSpec((None, tb, 2, W_BR), lambda i, j: (layer, i, 0, 0)),
        pl.BlockSpec((None, tb, S5_FLAT), lambda i, j: (layer, i, 0)),
        pl.BlockSpec((None, tb, S5_FLAT), lambda i, j: (layer, i, 0)),
        lw((1, D_MODEL)), lw((D_MODEL, _A_COLS)), lw((3, W_BR)), lw((1, S5_FLAT)), lw((1, S5_FLAT)),
        lw((2, 2, 256, 1024)), lw((2, 2, 1024, 256)), lw((1, W_BR)), lw((W_BR, W_BR)),
        lw((1, W_BR)), lw((2, W_BR, D_MODEL)),
    ]
    out_specs = [
        pl.BlockSpec((tb, tl, D_MODEL), lambda i, j: (i, j, 0)),
        pl.BlockSpec((tb, 2, W_BR), lambda i, j: (i, 0, 0)),
        pl.BlockSpec((tb, S5_FLAT), lambda i, j: (i, 0)),
        pl.BlockSpec((tb, S5_FLAT), lambda i, j: (i, 0)),
    ]
    out_shape = [
        jax.ShapeDtypeStruct((bsz, seq, D_MODEL), _F32),
        jax.ShapeDtypeStruct((bsz, 2, W_BR), _F32),
        jax.ShapeDtypeStruct((bsz, S5_FLAT), _F32),
        jax.ShapeDtypeStruct((bsz, S5_FLAT), _F32),
    ]
    scratch = [
        pltpu.VMEM((W_BR // LANES, rows, LANES), _F32),
        pltpu.VMEM((rows, S5_FLAT), _F32),
        pltpu.VMEM((rows, S5_FLAT), _F32),
        pltpu.VMEM((W_BR // LANES, rows, LANES), _F32),
    ]
    return pl.pallas_call(
        functools.partial(_mix_ab_kernel, tb=tb, tl=tl),
        grid=grid, in_specs=in_specs, out_specs=out_specs, out_shape=out_shape,
        scratch_shapes=scratch, name="mix_ab",
        compiler_params=pltpu.CompilerParams(
            dimension_semantics=("arbitrary", "arbitrary"), vmem_limit_bytes=VMEM_LIMIT_BYTES),
    )(x, st_conv, st_re, st_im, p["norm_w"], p["w_a"], p["conv_w"], p["a_r"], p["a_i"], p["bdb"],
      p["bdc"], p["s5_d"], p["w_glu"], p["b_glu"], p["w_br_ab"])


def _mix_cd_kernel(x_ref, mab_ref, hg_ref, gl_ref, nw_ref, w_ref, loglb_ref, log1m_ref, onem_ref,
                   hgn_ref, wgk_ref, bgk_ref, gscale_ref, gln_ref, wbr_ref, wout_ref, fn_ref,
                   hg_buf_ref, gl_buf_ref, *rest, tb, tl, chunk, seg, last):
    del hg_buf_ref, gl_buf_ref
    if last:
        xo_ref, yo_ref, hgo_ref, glo_ref = rest[:4]
        scr = rest[4:]
    else:
        xo_ref, hgo_ref, glo_ref = rest[:3]
        yo_ref = None
        scr = rest[3:]
    q_ref, k_ref, lf_ref, v_ref, g_ref, o_ref, h_ref = scr
    rows = tb * tl
    nchunk = rows // chunk
    nseg = chunk // seg
    step = pl.program_id(1)

    @pl.when(step == 0)
    def _():
        hgo_ref[...] = hg_ref[...]
        glo_ref[...] = gl_ref[...]

    h_ref[...] = (_rms_scale(x_ref[...].reshape(rows, D_MODEL)) * nw_ref[...]).astype(_BF)

    def proj(c0, n):
        return _dot(h_ref[...], w_ref[:, c0:c0 + n])

    def put(ref, lo, val):
        ref[:, lo:lo + val.shape[1]] = val

    def hgrn_gate(lo):
        def run():
            cs = slice(lo, lo + _G_SLAB)
            cf = proj(_B_CF + lo, _G_SLAB)
            gate_b = log1m_ref[:, cs] + _log_sigmoid(cf)
            gate_a = loglb_ref[:, cs]
            put(lf_ref, lo, (jnp.maximum(gate_a, gate_b) + _softplus_neg_abs(gate_a - gate_b)) * _LOG2E)
            put(k_ref, lo, onem_ref[:, cs] * _sigmoid(-cf))
        return run

    def gla_gate():
        gk_lin = _dot(proj(_B_DR, LANES).astype(_BF), wgk_ref[...]) + bgk_ref[...]
        put(lf_ref, W_BR, _log_sigmoid(gk_lin) * gscale_ref[...])

    def plain(ref, dst, src, fn):
        def run():
            put(ref, dst, fn(proj(src, _G_SLAB)))
        return run

    stage1 = []
    for lo in range(0, W_BR, _G_SLAB):
        stage1.append(hgrn_gate(lo))
        stage1.append(plain(q_ref, lo, _B_CQ + lo, lambda t: _silu(t) * (HEAD_DIM ** -0.5)))
        stage1.append(plain(v_ref, lo, _B_CI + lo, lambda t: t))
    stage1.append(gla_gate)
    for lo in range(0, W_BR, _G_SLAB):
        stage1.append(plain(q_ref, W_BR + lo, _B_DQ + lo, lambda t: t * (GLA_DK ** -0.5)))
        stage1.append(plain(k_ref, W_BR + lo, _B_DK + lo, lambda t: t))
        stage1.append(plain(v_ref, W_BR + lo, _B_DV + lo, lambda t: t))
    n_recurrence_inputs = len(stage1)
    for lo in range(0, W_BR, _G_SLAB):
        stage1.append(plain(g_ref, _G_CZ + lo, _B_CZ + lo, _silu))
        stage1.append(plain(g_ref, _G_DZ + lo, _B_DZ + lo, _silu))
    for lo in range(0, 2 * D_MODEL, _G_SLAB):
        stage1.append(plain(g_ref, _G_GC + lo, _B_GC + lo, _sigmoid))
    for _ in range(n_recurrence_inputs):
        stage1.pop(0)()

    nblk = chunk // SUBLANES
    blk_per_seg = seg // SUBLANES
    sub_i = lax.broadcasted_iota(jnp.int32, (SUBLANES, HEAD_DIM), 0)
    row_diag = lax.broadcasted_iota(jnp.int32, (chunk, HEAD_DIM), 0) & (DIAG - 1)
    ri = lax.broadcasted_iota(jnp.int32, (chunk, chunk), 0)
    ci = lax.broadcasted_iota(jnp.int32, (chunk, chunk), 1)
    eye = (lax.broadcasted_iota(jnp.int32, (HEAD_DIM, HEAD_DIM), 0)
           == lax.broadcasted_iota(jnp.int32, (HEAD_DIM, HEAD_DIM), 1))
    levels = []
    m = DIAG
    while 2 * m <= seg:
        levels.append(m)
        m *= 2
    level_masks = []
    level_signs = []
    for m in levels:
        sh = (2 * m).bit_length() - 1
        level_masks.append(((ri >> sh) == (ci >> sh)) & ((ri & (2 * m - 1)) >= m)
                           & ((ci & (2 * m - 1)) < m))
        half = lax.broadcasted_iota(jnp.int32, (1, 2 * m, HEAD_DIM), 1) >= m
        level_signs.append(jnp.where(half, 1.0, -1.0).astype(_F32))
    diag_masks = [(ri - ci == dd) & ((ri & (DIAG - 1)) >= dd) for dd in range(DIAG)]

    def block_roll(a, shift):
        return jnp.concatenate(
            [pltpu.roll(a[j * SUBLANES:(j + 1) * SUBLANES, :], shift, axis=0)
             for j in range(nblk)], axis=0)

    def seg_cumsum(lf):
        out = []
        carry = None
        for j in range(nblk):
            blk = lf[j * SUBLANES:(j + 1) * SUBLANES, :]
            d = 1
            while d < SUBLANES:
                blk = blk + jnp.where(sub_i >= d, pltpu.roll(blk, d, axis=0), 0.0)
                d *= 2
            if j % blk_per_seg != 0:
                blk = blk + carry
            carry = blk[SUBLANES - 1:SUBLANES, :]
            out.append(blk)
        return jnp.concatenate(out, axis=0)

    def load_state(s, hd):
        if hd < HEADS:
            return hgo_ref[s, hd]
        return jnp.concatenate([glo_ref[s, hd - HEADS], jnp.zeros((GLA_DK, HEAD_DIM), _F32)], axis=0)

    def store_state(s, hd, val):
        if hd < HEADS:
            hgo_ref[s, hd] = val
        else:
            glo_ref[s, hd - HEADS] = val[0:GLA_DK, :]

    def prepare(c, hd):
        src = pl.ds(c * chunk, chunk)
        sl = slice(hd * HEAD_DIM, (hd + 1) * HEAD_DIM)
        q = q_ref[src, sl]
        k = k_ref[src, sl]
        v = v_ref[src, sl]
        b = seg_cumsum(lf_ref[src, sl])
        level_ops = []
        for m, sgn in zip(levels, level_signs):
            b3 = b.reshape(chunk // (2 * m), 2 * m, HEAD_DIM)
            dec = jnp.exp2((b3 - b3[:, m - 1:m, :]) * sgn).reshape(chunk, HEAD_DIM)
            level_ops.append(((q * dec).astype(_BF), (k * dec).astype(_BF)))
        diag_cols = []
        for dd in range(DIAG):
            if dd == 0:
                diag_cols.append(jnp.sum(q * k, axis=-1, keepdims=True))
            else:
                e = jnp.where(row_diag >= dd, b - block_roll(b, dd), 0.0)
                diag_cols.append(
                    jnp.sum(q * block_roll(k, dd) * jnp.exp2(e), axis=-1, keepdims=True))
        b_last = b.reshape(nseg, seg, HEAD_DIM)[:, seg - 1:seg, :]
        q_in = q * jnp.exp2(b)
        k_dec = k * jnp.exp2((b_last - b.reshape(nseg, seg, HEAD_DIM)).reshape(chunk, HEAD_DIM))
        dec_cols = [jnp.sum(jnp.where(eye, jnp.exp2(b[(s + 1) * seg - 1:(s + 1) * seg, :]), 0.0),
                            axis=1, keepdims=True) for s in range(nseg)]
        return level_ops, diag_cols, q_in, k_dec, v, dec_cols

    def finish(c, hd, prepared):
        level_ops, diag_cols, q_in, k_dec, v, dec_cols = prepared
        r0 = c * chunk
        sl = slice(hd * HEAD_DIM, (hd + 1) * HEAD_DIM)
        att = jnp.zeros((chunk, chunk), _F32)
        for (q_m, k_m), mask in zip(level_ops, level_masks):
            a_m = lax.dot_general(q_m, k_m, (((1,), (1,)), ((), ())), preferred_element_type=_F32)
            att = jnp.where(mask, a_m, att)
        for dd in range(DIAG):
            att = jnp.where(diag_masks[dd], diag_cols[dd], att)
        att = att.astype(_BF)

        def state_update(sidx, state, dec_col, kd_s, v_s):
            upd = lax.dot_general(kd_s, v_s, (((0,), (0,)), ((), ())), preferred_element_type=_F32)
            store_state(sidx, hd, state * dec_col + upd)

        if nseg == 1:
            state = load_state(0, hd)
            v_bf = v.astype(_BF)
            lhs = jnp.concatenate([q_in.astype(_BF), att], axis=1)
            rhs = jnp.concatenate([state.astype(_BF), v_bf], axis=0)
            o_ref[pl.ds(r0, chunk), sl] = _dot(lhs, rhs)
            state_update(0, state, dec_cols[0], k_dec.astype(_BF), v_bf)
        else:
            o_intra = _dot(att, v.astype(_BF))
            for s in range(nseg):
                rs = slice(s * seg, (s + 1) * seg)
                state = load_state(s, hd)
                q_s, kd_s, v_s = q_in[rs], k_dec[rs], v[rs]
                if seg < BF16_ROWS:
                    zpad = jnp.zeros((BF16_ROWS - seg, HEAD_DIM), _F32)
                    q_s = jnp.concatenate([q_s, zpad], axis=0)
                    kd_s = jnp.concatenate([kd_s, zpad], axis=0)
                    v_s = jnp.concatenate([v_s, zpad], axis=0)
                o_inter = _dot(q_s.astype(_BF), state.astype(_BF))[0:seg, :]
                o_ref[pl.ds(r0 + s * seg, seg), sl] = o_intra[rs] + o_inter
                state_update(s, state, dec_cols[s], kd_s.astype(_BF), v_s.astype(_BF))

    units = [(c, hd) for c in range(nchunk) for hd in range(N_GEN_HEADS)]
    prepared = prepare(*units[0])
    for n, unit in enumerate(units):
        following = prepare(*units[n + 1]) if n + 1 < len(units) else None
        finish(*unit, prepared)
        prepared = following
        if stage1:
            stage1.pop(0)()
    while stage1:
        stage1.pop(0)()


    def normed(lo, norm_row):
        parts = []
        for hd in range(lo, lo + HEADS):
            o_h = o_ref[:, hd * HEAD_DIM:(hd + 1) * HEAD_DIM]
            parts.append(_rms_scale(o_h) * norm_row)
        return jnp.concatenate(parts, axis=1)

    y_c = normed(0, hgn_ref[...]) * g_ref[:, _G_CZ:_G_CZ + W_BR]
    y_d = normed(HEADS, gln_ref[...]) * g_ref[:, _G_DZ:_G_DZ + W_BR]
    merged = (mab_ref[...].reshape(rows, D_MODEL)
              + g_ref[:, _G_GC:_G_GC + D_MODEL] * _dot(y_c.astype(_BF), wbr_ref[0])
              + g_ref[:, _G_GD:_G_GD + D_MODEL] * _dot(y_d.astype(_BF), wbr_ref[1]))
    x_new = x_ref[...].reshape(rows, D_MODEL) + _dot(merged.astype(_BF), wout_ref[...])
    xo_ref[...] = x_new.reshape(tb, tl, D_MODEL)
    if last:
        yo_ref[...] = (_rms_scale(x_new) * fn_ref[...]).reshape(tb, tl, D_MODEL)


def _mix_cd(x, mab, st_hg, st_gla, hg_buf, gl_buf, p, layer, *, tb, tl, chunk, seg, last):
    bsz, seq, _ = x.shape
    rows = tb * tl
    grid = (bsz // tb, seq // tl)

    def lw(shape):
        nd = len(shape)
        return pl.BlockSpec((None,) + shape, lambda i, j, nd=nd: (layer,) + (0,) * nd,
                            pipeline_mode=pl.Buffered(1))

    act = pl.BlockSpec((tb, tl, D_MODEL), lambda i, j: (i, j, 0))
    hg_block = (None, tb, HEADS, HEAD_DIM, HEAD_DIM)
    gl_block = (None, tb, HEADS, GLA_DK, HEAD_DIM)

    def state_map(i, j):
        return (layer, i, 0, 0, 0)

    any_spec = pl.BlockSpec(memory_space=pl.ANY)
    in_specs = [
        act, act,
        pl.BlockSpec(hg_block, state_map), pl.BlockSpec(gl_block, state_map),
        lw((1, D_MODEL)), lw((D_MODEL, _B_COLS)), lw((1, W_BR)), lw((1, W_BR)), lw((1, W_BR)),
        lw((1, HEAD_DIM)), lw((LANES, W_BR)), lw((1, W_BR)),
        pl.BlockSpec((1, W_BR), lambda i, j: (0, 0)),
        lw((1, HEAD_DIM)), lw((2, W_BR, D_MODEL)), lw((D_MODEL, D_MODEL)),
        pl.BlockSpec((1, D_MODEL), lambda i, j: (0, 0)),
        any_spec, any_spec,
    ]
    n_in = len(in_specs)
    act_shape = jax.ShapeDtypeStruct((bsz, seq, D_MODEL), _F32)
    n_act_out = 2 if last else 1
    out_specs = [act] * n_act_out + [pl.BlockSpec(hg_block, state_map),
                                     pl.BlockSpec(gl_block, state_map)]
    out_shape = [act_shape] * n_act_out + [
        jax.ShapeDtypeStruct(hg_buf.shape, _F32), jax.ShapeDtypeStruct(gl_buf.shape, _F32)]
    wide = N_GEN_HEADS * HEAD_DIM
    scratch = [pltpu.VMEM((rows, wide), _F32) for _ in range(4)] + [
        pltpu.VMEM((rows, _G_COLS), _F32),
        pltpu.VMEM((rows, wide), _F32),
        pltpu.VMEM((rows, D_MODEL), _BF)]
    return pl.pallas_call(
        functools.partial(_mix_cd_kernel, tb=tb, tl=tl, chunk=chunk, seg=seg, last=last),
        grid=grid, in_specs=in_specs, out_specs=out_specs, out_shape=out_shape,
        scratch_shapes=scratch, name="mix_cd",
        input_output_aliases={n_in - 2: n_act_out, n_in - 1: n_act_out + 1},
        compiler_params=pltpu.CompilerParams(
            dimension_semantics=("arbitrary", "arbitrary"), vmem_limit_bytes=VMEM_LIMIT_BYTES),
    )(x, mab, st_hg, st_gla, p["norm_w"], p["w_b"], p["log_lb"], p["log1m_lb"], p["onem_lb"],
      p["hgrn_norm"], p["w_gk"], p["b_gk"], p["gk_scale"], p["gla_norm"], p["w_br_cd"], p["w_out"],
      p["final_norm"], hg_buf, gl_buf)


def _pad_gla_heads(w):
    lead = w.shape[:-1]
    w4 = w.reshape(lead + (HEADS, GLA_DK))
    w4 = jnp.pad(w4, [(0, 0)] * len(lead) + [(0, 0), (0, HEAD_DIM - GLA_DK)])
    return w4.reshape(lead + (HEADS * HEAD_DIM,))


def _prep_params(norm_w, w_in, conv_w, s5_a_re, s5_a_im, s5_log_dt, s5_b_re, s5_b_im, s5_c_re,
                 s5_c_im, s5_d, w_glu, b_glu, hgrn_lb_raw, hgrn_norm, w_gk, b_gk, gla_norm,
                 w_branch, w_out, final_norm):
    f32 = _F32
    p = {}
    p["norm_w"] = norm_w.astype(f32)[:, None, :]
    p["w_a"] = jnp.concatenate(
        [w_in[:, :, 0:_OFF_CQ], w_in[:, :, _OFF_GATE:_OFF_GATE + 2 * D_MODEL]], axis=2).astype(_BF)
    p["w_b"] = jnp.concatenate(
        [w_in[:, :, _OFF_CQ:_OFF_DQ],
         _pad_gla_heads(w_in[:, :, _OFF_DQ:_OFF_DK]),
         _pad_gla_heads(w_in[:, :, _OFF_DK:_OFF_DV]),
         w_in[:, :, _OFF_DV:_OFF_DR],
         jnp.pad(w_in[:, :, _OFF_DR:_OFF_GATE], ((0, 0), (0, 0), (0, LANES - GLA_RANK))),
         w_in[:, :, _OFF_GATE + 2 * D_MODEL:_OFF_GATE + 4 * D_MODEL]], axis=2).astype(_BF)
    p["conv_w"] = conv_w.astype(f32)

    ar = s5_a_re.astype(f32)
    ai = s5_a_im.astype(f32)
    dt = jnp.exp(s5_log_dt.astype(f32))[..., None]
    mag = jnp.exp(dt * ar)
    abar_r = mag * jnp.cos(dt * ai)
    abar_i = mag * jnp.sin(dt * ai)
    den = ar * ar + ai * ai
    zr = ((abar_r - 1.0) * ar + abar_i * ai) / den
    zi = (abar_i * ar - (abar_r - 1.0) * ai) / den
    b_re = s5_b_re.astype(f32)
    b_im = s5_b_im.astype(f32)
    bbar_r = zr[..., None] * b_re - zi[..., None] * b_im
    bbar_i = zr[..., None] * b_im + zi[..., None] * b_re
    eye = jnp.eye(S5_GROUP, dtype=f32)

    def bd_in(bb):
        x = bb.reshape(DEPTH, 2, S5_GROUP, S5_STATE, S5_GROUP)
        x = jnp.transpose(x, (0, 1, 2, 4, 3))[:, :, :, :, None, :]
        x = x * eye[None, None, :, None, :, None]
        return x.reshape(DEPTH, 2, 256, 1024)

    def bd_out(c):
        x = c.reshape(DEPTH, 2, S5_GROUP, S5_GROUP, S5_STATE)
        x = jnp.transpose(x, (0, 1, 2, 4, 3))[:, :, :, :, None, :]
        x = x * eye[None, None, :, None, :, None]
        return x.reshape(DEPTH, 2, 1024, 256)

    p["bdb"] = jnp.stack([bd_in(bbar_r), bd_in(bbar_i)], axis=2).astype(_BF)
    p["bdc"] = jnp.stack([bd_out(s5_c_re.astype(f32)), bd_out(-s5_c_im.astype(f32))],
                         axis=2).astype(_BF)
    p["a_r"] = abar_r.reshape(DEPTH, 1, S5_FLAT)
    p["a_i"] = abar_i.reshape(DEPTH, 1, S5_FLAT)
    p["s5_d"] = s5_d.astype(f32)[:, None, :]
    p["w_glu"] = w_glu.astype(_BF)
    p["b_glu"] = b_glu.astype(f32)[:, None, :]
    p["w_br_ab"] = w_branch[:, 0:2].astype(_BF)
    p["w_br_cd"] = w_branch[:, 2:4].astype(_BF)
    p["w_out"] = w_out.astype(_BF)

    lb_cum = jnp.cumsum(jax.nn.softmax(hgrn_lb_raw.astype(f32), axis=0), axis=0)
    lb = lb_cum - lb_cum[0:1]
    p["log_lb"] = jnp.log(lb)[:, None, :]
    p["log1m_lb"] = jnp.log1p(-lb)[:, None, :]
    p["onem_lb"] = (1.0 - lb)[:, None, :]
    p["hgrn_norm"] = hgrn_norm.astype(f32)[:, None, :]
    p["gla_norm"] = gla_norm.astype(f32)[:, None, :]
    p["w_gk"] = jnp.pad(_pad_gla_heads(w_gk), ((0, 0), (0, LANES - GLA_RANK), (0, 0))).astype(_BF)
    p["b_gk"] = _pad_gla_heads(b_gk.astype(f32))[:, None, :]
    p["gk_scale"] = _pad_gla_heads(jnp.full((1, HEADS * GLA_DK), _LOG2E / GLA_GATE_NORM, f32))
    p["final_norm"] = final_norm.astype(f32)[None, :]
    return p


def _trunk(x, st_conv, st_re, st_im, st_hg, st_gla, p, *, ab_cfg, cd_cfg):
    bsz = x.shape[0]
    new_conv, new_re, new_im = [], [], []
    y = None
    st_re = st_re.reshape(DEPTH, bsz, S5_FLAT)
    st_im = st_im.reshape(DEPTH, bsz, S5_FLAT)
    new_hg = jnp.zeros(st_hg.shape, _F32)
    new_gla = jnp.zeros(st_gla.shape, _F32)
    for layer in range(DEPTH):
        mab, cbuf, h_r, h_i = _mix_ab(x, st_conv, st_re, st_im, p, layer, **ab_cfg)
        last = layer == DEPTH - 1
        outs = _mix_cd(x, mab, st_hg, st_gla, new_hg, new_gla, p, layer, last=last, **cd_cfg)
        if last:
            x, y, new_hg, new_gla = outs
        else:
            x, new_hg, new_gla = outs
        new_conv.append(cbuf)
        new_re.append(h_r.reshape(bsz, S5_GROUPS, S5_STATE))
        new_im.append(h_i.reshape(bsz, S5_GROUPS, S5_STATE))
    return (y, jnp.stack(new_conv, 0), jnp.stack(new_re, 0), jnp.stack(new_im, 0), new_hg, new_gla)


_PROMPT_AB = dict(tb=8, tl=32)
_PROMPT_CD = dict(tb=1, tl=256, chunk=128, seg=128)
_SAMPLE_AB = dict(tb=32, tl=8)
_SAMPLE_CD = dict(tb=16, tl=8, chunk=128, seg=8)


@jax.jit
def _forward(x_prompt, x_sample, state_conv, state_ssm_re, state_ssm_im, state_hgrn, state_gla,
             *params):
    p = _prep_params(*params)
    bp = x_prompt.shape[0]
    dt = x_prompt.dtype
    z_conv = jnp.zeros((DEPTH, bp, 2, W_BR), dt)
    z_ssm = jnp.zeros((DEPTH, bp, S5_GROUPS, S5_STATE), dt)
    z_hg = jnp.zeros((DEPTH, bp, HEADS, HEAD_DIM, HEAD_DIM), dt)
    z_gla = jnp.zeros((DEPTH, bp, HEADS, GLA_DK, HEAD_DIM), dt)
    y_p, conv_p, re_p, im_p, hg_p, gla_p = _trunk(
        x_prompt, z_conv, z_ssm, z_ssm, z_hg, z_gla, p, ab_cfg=_PROMPT_AB, cd_cfg=_PROMPT_CD)
    y_s, conv_s, re_s, im_s, hg_s, gla_s = _trunk(
        x_sample, state_conv, state_ssm_re, state_ssm_im, state_hgrn, state_gla, p,
        ab_cfg=_SAMPLE_AB, cd_cfg=_SAMPLE_CD)
    return (y_p, y_s, conv_p, conv_s, re_p, re_s, im_p, im_s, hg_p, hg_s, gla_p, gla_s)


def kernel(x_prompt, x_sample, state_conv, state_ssm_re, state_ssm_im, state_hgrn, state_gla,
           norm_w, w_in, conv_w, s5_a_re, s5_a_im, s5_log_dt, s5_b_re, s5_b_im, s5_c_re, s5_c_im,
           s5_d, w_glu, b_glu, hgrn_lb_raw, hgrn_norm, w_gk, b_gk, gla_norm, w_branch, w_out,
           final_norm):
    return _forward(x_prompt, x_sample, state_conv, state_ssm_re, state_ssm_im, state_hgrn,
                    state_gla, norm_w, w_in, conv_w, s5_a_re, s5_a_im, s5_log_dt, s5_b_re, s5_b_im,
                    s5_c_re, s5_c_im, s5_d, w_glu, b_glu, hgrn_lb_raw, hgrn_norm, w_gk, b_gk,
                    gla_norm, w_branch, w_out, final_norm)
```

```python
import functools
import math

import jax
import jax.numpy as jnp
import numpy as np
from jax import lax
from jax.experimental import pallas as pl
from jax.experimental.pallas import tpu as pltpu

D_MODEL = 1024
DEPTH = 4
W_BR = 512
S5_GROUPS = 32
S5_GROUP = 16
S5_STATE = 64
S5_FLAT = S5_GROUPS * S5_STATE
HEADS = 4
HEAD_DIM = 128
GLA_DK = 64
GLA_RANK = 16
GLA_GATE_NORM = 16.0
EPS = 1e-6
N_GEN_HEADS = 2 * HEADS
DIAG = 4
SUBLANES = 8
LANES = 128
BF16_ROWS = 16
VMEM_LIMIT_BYTES = 56 * 1024 * 1024

_OFF_CQ = 3072
_OFF_DQ = 5120
_OFF_DK = 5376
_OFF_DV = 5632
_OFF_DR = 6656
_OFF_GATE = 6672

_A_AX, _A_AB, _A_AC, _A_AZ, _A_SU, _A_SZ, _A_GA, _A_GB = 0, 512, 1024, 1536, 2048, 2560, 3072, 4096
_A_COLS = 5120
_B_CQ, _B_CF, _B_CI, _B_CZ, _B_DQ, _B_DK, _B_DV, _B_DZ, _B_DR, _B_GC, _B_GD = (
    0, 512, 1024, 1536, 2048, 2560, 3072, 3584, 4096, 4224, 5248)
_B_COLS = 6272
_G_CZ, _G_DZ, _G_GC, _G_GD = 0, 512, 1024, 2048
_G_COLS = 3072
_G_SLAB = 256

_GELU_C = math.sqrt(2.0 / math.pi)
_LOG2E = math.log2(math.e)
_LN2 = math.log(2.0)
_BF = jnp.bfloat16
_F32 = jnp.float32


def _sigmoid(x):
    return 1.0 / (1.0 + jnp.exp(-x))


def _silu(x):
    return x * _sigmoid(x)


def _softplus_neg_abs(x):
    return jnp.log2(1.0 + jnp.exp2(jnp.abs(x) * (-_LOG2E))) * _LN2


def _log_sigmoid(x):
    return jnp.minimum(x, 0.0) - _softplus_neg_abs(x)


def _gelu_tanh(x):
    return 0.5 * x * (1.0 + jnp.tanh(_GELU_C * (x + 0.044715 * (x * x * x))))


def _rms_scale(x):
    return x * lax.rsqrt(jnp.mean(x * x, axis=-1, keepdims=True) + EPS)


def _dot(a, b):
    return jnp.dot(a, b, preferred_element_type=_F32)


def _mix_ab_kernel(x_ref, cs_ref, sr_ref, si_ref, nw_ref, w_ref, cw_ref, ar_ref, ai_ref, bdb_ref,
                   bdc_ref, d_ref, wglu_ref, bglu_ref, wbr_ref,
                   m_ref, cso_ref, sro_ref, sio_ref,
                   u_ref, bur_ref, bui_ref, y_ref, *, tb, tl):
    rows = tb * tl
    ngroup = tb // SUBLANES
    step = pl.program_id(1)

    @pl.when(step == 0)
    def _():
        cso_ref[...] = cs_ref[...]
        sro_ref[...] = sr_ref[...]
        sio_ref[...] = si_ref[...]

    x = x_ref[...].reshape(rows, D_MODEL)
    h = (_rms_scale(x) * nw_ref[...]).astype(_BF)

    def proj(c0, n):
        return _dot(h, w_ref[:, c0:c0 + n])

    v2 = proj(_A_AC, W_BR) * proj(_A_AX, W_BR)
    v3 = v2.reshape(tb, tl, W_BR)
    t_idx = lax.broadcasted_iota(jnp.int32, (tb, tl, W_BR), 1)
    buf = cso_ref[...]
    b0 = buf[:, 0:1, :]
    b1 = buf[:, 1:2, :]
    r1 = pltpu.roll(v2, 1, axis=0).reshape(tb, tl, W_BR)
    r2 = pltpu.roll(v2, 2, axis=0).reshape(tb, tl, W_BR)
    p1 = jnp.where(t_idx == 0, b1, r1)
    p2 = jnp.where(t_idx == 0, b0, jnp.where(t_idx == 1, b1, r2))
    cw = cw_ref[...]
    conv = p2 * cw[0:1, :] + p1 * cw[1:2, :] + v3 * cw[2:3, :]
    cso_ref[...] = v3[:, tl - 2:tl, :]
    y_a = proj(_A_AB, W_BR) * conv.reshape(rows, W_BR) * _silu(proj(_A_AZ, W_BR))

    u = proj(_A_SU, W_BR)
    nslab = W_BR // LANES
    for c in range(nslab):
        u_ref[c] = u[:, c * LANES:(c + 1) * LANES]
    up = jnp.concatenate(
        [jnp.concatenate(
            [u_ref[c, pl.ds(g * SUBLANES * tl + t, SUBLANES, stride=tl), :]
             for g in range(ngroup) for t in range(tl)], axis=0)
         for c in range(nslab)], axis=1).astype(_BF)
    for hh in range(2):
        uh = up[:, 256 * hh:256 * hh + 256]
        bur_ref[:, 1024 * hh:1024 * hh + 1024] = _dot(uh, bdb_ref[hh, 0])
        bui_ref[:, 1024 * hh:1024 * hh + 1024] = _dot(uh, bdb_ref[hh, 1])

    lane_chunk = 1024
    for g in range(ngroup):
        base = g * SUBLANES * tl
        srow = slice(g * SUBLANES, (g + 1) * SUBLANES)
        for lc in range(S5_FLAT // lane_chunk):
            ls = slice(lc * lane_chunk, (lc + 1) * lane_chunk)
            a_r = jnp.broadcast_to(ar_ref[:, ls], (SUBLANES, lane_chunk))
            a_i = jnp.broadcast_to(ai_ref[:, ls], (SUBLANES, lane_chunk))

            hr, hi = sro_ref[srow, ls], sio_ref[srow, ls]
            for t in range(tl):
                rs = slice(base + t * SUBLANES, base + (t + 1) * SUBLANES)
                hr, hi = (a_r * hr - a_i * hi + bur_ref[rs, ls],
                          a_r * hi + a_i * hr + bui_ref[rs, ls])
                bur_ref[rs, ls] = hr
                bui_ref[rs, ls] = hi
            sro_ref[srow, ls] = hr
            sio_ref[srow, ls] = hi

    for hh in range(2):
        cs = slice(1024 * hh, 1024 * hh + 1024)
        y_h = (_dot(bur_ref[:, cs].astype(_BF), bdc_ref[hh, 0])
               + _dot(bui_ref[:, cs].astype(_BF), bdc_ref[hh, 1]))
        y_ref[2 * hh] = y_h[:, 0:LANES]
        y_ref[2 * hh + 1] = y_h[:, LANES:2 * LANES]
    ys = jnp.concatenate(
        [jnp.concatenate(
            [y_ref[c, pl.ds(g * SUBLANES * tl + b8, tl, stride=SUBLANES), :]
             for g in range(ngroup) for b8 in range(SUBLANES)], axis=0)
         for c in range(nslab)], axis=1)
    y_s = ys + d_ref[...] * u
    sg = _gelu_tanh(y_s)
    glu = sg * _sigmoid(_dot(sg.astype(_BF), wglu_ref[...]) + bglu_ref[...])
    y_b = glu * _silu(proj(_A_SZ, W_BR))

    merged = (_sigmoid(proj(_A_GA, D_MODEL)) * _dot(y_a.astype(_BF), wbr_ref[0])
              + _sigmoid(proj(_A_GB, D_MODEL)) * _dot(y_b.astype(_BF), wbr_ref[1]))
    m_ref[...] = merged.reshape(tb, tl, D_MODEL)


def _mix_ab(x, st_conv, st_re, st_im, p, layer, *, tb, tl):
    bsz, seq, _ = x.shape
    rows = tb * tl
    grid = (bsz // tb, seq // tl)

    def lw(shape):
        nd = len(shape)
        return pl.BlockSpec((None,) + shape, lambda i, j, nd=nd: (layer,) + (0,) * nd,
                            pipeline_mode=pl.Buffered(1))

    in_specs = [
        pl.BlockSpec((tb, tl, D_MODEL), lambda i, j: (i, j, 0)),
        pl.BlockSpec((None, tb, 2, W_BR), lambda i, j: (layer, i, 0, 0)),
        pl.BlockSpec((None, tb, S5_FLAT), lambda i, j: (layer, i, 0)),
        pl.BlockSpec((None, tb, S5_FLAT), lambda i, j: (layer, i, 0)),
        lw((1, D_MODEL)), lw((D_MODEL, _A_COLS)), lw((3, W_BR)), lw((1, S5_FLAT)), lw((1, S5_FLAT)),
        lw((2, 2, 256, 1024)), lw((2, 2, 1024, 256)), lw((1, W_BR)), lw((W_BR, W_BR)),
        lw((1, W_BR)), lw((2, W_BR, D_MODEL)),
    ]
    out_specs = [
        pl.BlockSpec((tb, tl, D_MODEL), lambda i, j: (i, j, 0)),
        pl.BlockSpec((tb, 2, W_BR), lambda i, j: (i, 0, 0)),
        pl.BlockSpec((tb, S5_FLAT), lambda i, j: (i, 0)),
        pl.BlockSpec((tb, S5_FLAT), lambda i, j: (i, 0)),
    ]
    out_shape = [
        jax.ShapeDtypeStruct((bsz, seq, D_MODEL), _F32),
        jax.ShapeDtypeStruct((bsz, 2, W_BR), _F32),
        jax.ShapeDtypeStruct((bsz, S5_FLAT), _F32),
        jax.ShapeDtypeStruct((bsz, S5_FLAT), _F32),
    ]
    scratch = [
        pltpu.VMEM((W_BR // LANES, rows, LANES), _F32),
        pltpu.VMEM((rows, S5_FLAT), _F32),
        pltpu.VMEM((rows, S5_FLAT), _F32),
        pltpu.VMEM((W_BR // LANES, rows, LANES), _F32),
    ]
    return pl.pallas_call(
        functools.partial(_mix_ab_kernel, tb=tb, tl=tl),
        grid=grid, in_specs=in_specs, out_specs=out_specs, out_shape=out_shape,
        scratch_shapes=scratch, name="mix_ab",
        compiler_params=pltpu.CompilerParams(
            dimension_semantics=("arbitrary", "arbitrary"), vmem_limit_bytes=VMEM_LIMIT_BYTES),
    )(x, st_conv, st_re, st_im, p["norm_w"], p["w_a"], p["conv_w"], p["a_r"], p["a_i"], p["bdb"],
      p["bdc"], p["s5_d"], p["w_glu"], p["b_glu"], p["w_br_ab"])


def _att_levels(seg):
    levels = []
    m = DIAG
    while 2 * m <= seg:
        levels.append(m)
        m *= 2
    return levels


def _att_source(chunk, seg):
    ri, ci = np.indices((chunk, chunk))
    levels = _att_levels(seg)
    code = np.zeros((chunk, chunk), np.int32)
    for idx, m in enumerate(levels):
        same = (ri // (2 * m)) == (ci // (2 * m))
        code[same & (ri % (2 * m) >= m) & (ci % (2 * m) < m)] = 1 + idx
    for dd in range(DIAG):
        code[(ri - ci == dd) & (ri % DIAG >= dd)] = 1 + len(levels) + dd
    return code


def _mix_cd_kernel(x_ref, mab_ref, hg_ref, gl_ref, nw_ref, w_ref, loglb_ref, log1m_ref, onem_ref,
                   hgn_ref, wgk_ref, bgk_ref, gscale_ref, gln_ref, wbr_ref, wout_ref, fn_ref,
                   src_ref, hg_buf_ref, gl_buf_ref, *rest, tb, tl, chunk, seg, last):
    del hg_buf_ref, gl_buf_ref
    if last:
        xo_ref, yo_ref, hgo_ref, glo_ref = rest[:4]
        scr = rest[4:]
    else:
        xo_ref, hgo_ref, glo_ref = rest[:3]
        yo_ref = None
        scr = rest[3:]
    q_ref, k_ref, lf_ref, v_ref, g_ref, o_ref, h_ref = scr
    rows = tb * tl
    nchunk = rows // chunk
    nseg = chunk // seg
    step = pl.program_id(1)

    @pl.when(step == 0)
    def _():
        hgo_ref[...] = hg_ref[...]
        glo_ref[...] = gl_ref[...]

    h_ref[...] = (_rms_scale(x_ref[...].reshape(rows, D_MODEL)) * nw_ref[...]).astype(_BF)

    def proj(c0, n):
        return _dot(h_ref[...], w_ref[:, c0:c0 + n])

    def put(ref, lo, val):
        ref[:, lo:lo + val.shape[1]] = val

    def hgrn_gate(lo):
        def run():
            cs = slice(lo, lo + _G_SLAB)
            cf = proj(_B_CF + lo, _G_SLAB)
            gate_b = log1m_ref[:, cs] + _log_sigmoid(cf)
            gate_a = loglb_ref[:, cs]
            put(lf_ref, lo, (jnp.maximum(gate_a, gate_b) + _softplus_neg_abs(gate_a - gate_b)) * _LOG2E)
            put(k_ref, lo, onem_ref[:, cs] * _sigmoid(-cf))
        return run

    def gla_gate():
        gk_lin = _dot(proj(_B_DR, LANES).astype(_BF), wgk_ref[...]) + bgk_ref[...]
        put(lf_ref, W_BR, _log_sigmoid(gk_lin) * gscale_ref[...])

    def plain(ref, dst, src, fn):
        def run():
            put(ref, dst, fn(proj(src, _G_SLAB)))
        return run

    stage1 = []
    for lo in range(0, W_BR, _G_SLAB):
        stage1.append(hgrn_gate(lo))
        stage1.append(plain(q_ref, lo, _B_CQ + lo, lambda t: _silu(t) * (HEAD_DIM ** -0.5)))
        stage1.append(plain(v_ref, lo, _B_CI + lo, lambda t: t))
    stage1.append(gla_gate)
    for lo in range(0, W_BR, _G_SLAB):
        stage1.append(plain(q_ref, W_BR + lo, _B_DQ + lo, lambda t: t * (GLA_DK ** -0.5)))
        stage1.append(plain(k_ref, W_BR + lo, _B_DK + lo, lambda t: t))
        stage1.append(plain(v_ref, W_BR + lo, _B_DV + lo, lambda t: t))
    n_recurrence_inputs = len(stage1)
    for lo in range(0, W_BR, _G_SLAB):
        stage1.append(plain(g_ref, _G_CZ + lo, _B_CZ + lo, _silu))
        stage1.append(plain(g_ref, _G_DZ + lo, _B_DZ + lo, _silu))
    for lo in range(0, 2 * D_MODEL, _G_SLAB):
        stage1.append(plain(g_ref, _G_GC + lo, _B_GC + lo, _sigmoid))
    for _ in range(n_recurrence_inputs):
        stage1.pop(0)()

    nblk = chunk // SUBLANES
    blk_per_seg = seg // SUBLANES
    sub_i = lax.broadcasted_iota(jnp.int32, (SUBLANES, HEAD_DIM), 0)
    row_diag = lax.broadcasted_iota(jnp.int32, (chunk, HEAD_DIM), 0) & (DIAG - 1)
    eye = (lax.broadcasted_iota(jnp.int32, (HEAD_DIM, HEAD_DIM), 0)
           == lax.broadcasted_iota(jnp.int32, (HEAD_DIM, HEAD_DIM), 1))
    levels = _att_levels(seg)
    level_signs = []
    for m in levels:
        half = lax.broadcasted_iota(jnp.int32, (1, 2 * m, HEAD_DIM), 1) >= m
        level_signs.append(jnp.where(half, 1.0, -1.0).astype(_F32))

    def block_roll(a, shift):
        return jnp.concatenate(
            [pltpu.roll(a[j * SUBLANES:(j + 1) * SUBLANES, :], shift, axis=0)
             for j in range(nblk)], axis=0)

    def seg_cumsum(lf):
        out = []
        carry = None
        for j in range(nblk):
            blk = lf[j * SUBLANES:(j + 1) * SUBLANES, :]
            d = 1
            while d < SUBLANES:
                blk = blk + jnp.where(sub_i >= d, pltpu.roll(blk, d, axis=0), 0.0)
                d *= 2
            if j % blk_per_seg != 0:
                blk = blk + carry
            carry = blk[SUBLANES - 1:SUBLANES, :]
            out.append(blk)
        return jnp.concatenate(out, axis=0)

    def load_state(s, hd):
        if hd < HEADS:
            return hgo_ref[s, hd]
        return jnp.concatenate([glo_ref[s, hd - HEADS], jnp.zeros((GLA_DK, HEAD_DIM), _F32)], axis=0)

    def store_state(s, hd, val):
        if hd < HEADS:
            hgo_ref[s, hd] = val
        else:
            glo_ref[s, hd - HEADS] = val[0:GLA_DK, :]

    def prepare(c, hd):
        src = pl.ds(c * chunk, chunk)
        sl = slice(hd * HEAD_DIM, (hd + 1) * HEAD_DIM)
        q = q_ref[src, sl]
        k = k_ref[src, sl]
        v = v_ref[src, sl]
        b = seg_cumsum(lf_ref[src, sl])
        level_ops = []
        q_bf = q.astype(_BF)
        k_bf = k.astype(_BF)
        for m, sgn in zip(levels, level_signs):
            b3 = b.reshape(chunk // (2 * m), 2 * m, HEAD_DIM)
            dec = jnp.exp2((b3 - b3[:, m - 1:m, :]) * sgn).reshape(chunk, HEAD_DIM).astype(_BF)
            level_ops.append((q_bf * dec, k_bf * dec))
        diag_cols = []
        for dd in range(DIAG):
            if dd == 0:
                diag_cols.append(jnp.sum(q * k, axis=-1, keepdims=True))
            else:
                e = jnp.where(row_diag >= dd, b - block_roll(b, dd), 0.0)
                diag_cols.append(
                    jnp.sum(q * block_roll(k, dd) * jnp.exp2(e), axis=-1, keepdims=True))
        b_last = b.reshape(nseg, seg, HEAD_DIM)[:, seg - 1:seg, :]
        q_in = q * jnp.exp2(b)
        k_dec = k * jnp.exp2((b_last - b.reshape(nseg, seg, HEAD_DIM)).reshape(chunk, HEAD_DIM))
        dec_cols = [jnp.sum(jnp.where(eye, jnp.exp2(b[(s + 1) * seg - 1:(s + 1) * seg, :]), 0.0),
                            axis=1, keepdims=True) for s in range(nseg)]
        return level_ops, diag_cols, q_in, k_dec, v, dec_cols

    def finish(c, hd, prepared):
        level_ops, diag_cols, q_in, k_dec, v, dec_cols = prepared
        r0 = c * chunk
        sl = slice(hd * HEAD_DIM, (hd + 1) * HEAD_DIM)
        att = jnp.zeros((chunk, chunk), _F32)
        for idx, (q_m, k_m) in enumerate(level_ops):
            a_m = lax.dot_general(q_m, k_m, (((1,), (1,)), ((), ())), preferred_element_type=_F32)
            att = jnp.where(src_ref[...] == 1 + idx, a_m, att)
        for dd in range(DIAG):
            att = jnp.where(src_ref[...] == 1 + len(levels) + dd, diag_cols[dd], att)
        att = att.astype(_BF)

        def state_update(sidx, state, dec_col, kd_s, v_s):
            upd = lax.dot_general(kd_s, v_s, (((0,), (0,)), ((), ())), preferred_element_type=_F32)
            store_state(sidx, hd, state * dec_col + upd)

        if nseg == 1:
            state = load_state(0, hd)
            v_bf = v.astype(_BF)
            lhs = jnp.concatenate([q_in.astype(_BF), att], axis=1)
            rhs = jnp.concatenate([state.astype(_BF), v_bf], axis=0)
            o_ref[pl.ds(r0, chunk), sl] = _dot(lhs, rhs)
            state_update(0, state, dec_cols[0], k_dec.astype(_BF), v_bf)
        else:
            o_intra = _dot(att, v.astype(_BF))
            for s in range(nseg):
                rs = slice(s * seg, (s + 1) * seg)
                state = load_state(s, hd)
                q_s, kd_s, v_s = q_in[rs], k_dec[rs], v[rs]
                if seg < BF16_ROWS:
                    zpad = jnp.zeros((BF16_ROWS - seg, HEAD_DIM), _F32)
                    q_s = jnp.concatenate([q_s, zpad], axis=0)
                    kd_s = jnp.concatenate([kd_s, zpad], axis=0)
                    v_s = jnp.concatenate([v_s, zpad], axis=0)
                o_inter = _dot(q_s.astype(_BF), state.astype(_BF))[0:seg, :]
                o_ref[pl.ds(r0 + s * seg, seg), sl] = o_intra[rs] + o_inter
                state_update(s, state, dec_cols[s], kd_s.astype(_BF), v_s.astype(_BF))

    units = [(c, hd) for c in range(nchunk) for hd in range(N_GEN_HEADS)]
    prepared = prepare(*units[0])
    for n, unit in enumerate(units):
        following = prepare(*units[n + 1]) if n + 1 < len(units) else None
        finish(*unit, prepared)
        prepared = following
        if stage1:
            stage1.pop(0)()
    while stage1:
        stage1.pop(0)()


    def normed(lo, norm_row):
        parts = []
        for hd in range(lo, lo + HEADS):
            o_h = o_ref[:, hd * HEAD_DIM:(hd + 1) * HEAD_DIM]
            parts.append(_rms_scale(o_h) * norm_row)
        return jnp.concatenate(parts, axis=1)

    y_c = normed(0, hgn_ref[...]) * g_ref[:, _G_CZ:_G_CZ + W_BR]
    y_d = normed(HEADS, gln_ref[...]) * g_ref[:, _G_DZ:_G_DZ + W_BR]
    merged = (mab_ref[...].reshape(rows, D_MODEL)
              + g_ref[:, _G_GC:_G_GC + D_MODEL] * _dot(y_c.astype(_BF), wbr_ref[0])
              + g_ref[:, _G_GD:_G_GD + D_MODEL] * _dot(y_d.astype(_BF), wbr_ref[1]))
    x_new = x_ref[...].reshape(rows, D_MODEL) + _dot(merged.astype(_BF), wout_ref[...])
    xo_ref[...] = x_new.reshape(tb, tl, D_MODEL)
    if last:
        yo_ref[...] = (_rms_scale(x_new) * fn_ref[...]).reshape(tb, tl, D_MODEL)


def _mix_cd(x, mab, st_hg, st_gla, hg_buf, gl_buf, p, layer, *, tb, tl, chunk, seg, last):
    bsz, seq, _ = x.shape
    rows = tb * tl
    grid = (bsz // tb, seq // tl)

    def lw(shape):
        nd = len(shape)
        return pl.BlockSpec((None,) + shape, lambda i, j, nd=nd: (layer,) + (0,) * nd,
                            pipeline_mode=pl.Buffered(1))

    act = pl.BlockSpec((tb, tl, D_MODEL), lambda i, j: (i, j, 0))
    hg_block = (None, tb, HEADS, HEAD_DIM, HEAD_DIM)
    gl_block = (None, tb, HEADS, GLA_DK, HEAD_DIM)

    def state_map(i, j):
        return (layer, i, 0, 0, 0)

    any_spec = pl.BlockSpec(memory_space=pl.ANY)
    in_specs = [
        act, act,
        pl.BlockSpec(hg_block, state_map), pl.BlockSpec(gl_block, state_map),
        lw((1, D_MODEL)), lw((D_MODEL, _B_COLS)), lw((1, W_BR)), lw((1, W_BR)), lw((1, W_BR)),
        lw((1, HEAD_DIM)), lw((LANES, W_BR)), lw((1, W_BR)),
        pl.BlockSpec((1, W_BR), lambda i, j: (0, 0)),
        lw((1, HEAD_DIM)), lw((2, W_BR, D_MODEL)), lw((D_MODEL, D_MODEL)),
        pl.BlockSpec((1, D_MODEL), lambda i, j: (0, 0)),
        pl.BlockSpec((chunk, chunk), lambda i, j: (0, 0), pipeline_mode=pl.Buffered(1)),
        any_spec, any_spec,
    ]
    n_in = len(in_specs)
    act_shape = jax.ShapeDtypeStruct((bsz, seq, D_MODEL), _F32)
    n_act_out = 2 if last else 1
    out_specs = [act] * n_act_out + [pl.BlockSpec(hg_block, state_map),
                                     pl.BlockSpec(gl_block, state_map)]
    out_shape = [act_shape] * n_act_out + [
        jax.ShapeDtypeStruct(hg_buf.shape, _F32), jax.ShapeDtypeStruct(gl_buf.shape, _F32)]
    wide = N_GEN_HEADS * HEAD_DIM
    scratch = [pltpu.VMEM((rows, wide), _F32) for _ in range(4)] + [
        pltpu.VMEM((rows, _G_COLS), _F32),
        pltpu.VMEM((rows, wide), _F32),
        pltpu.VMEM((rows, D_MODEL), _BF)]
    return pl.pallas_call(
        functools.partial(_mix_cd_kernel, tb=tb, tl=tl, chunk=chunk, seg=seg, last=last),
        grid=grid, in_specs=in_specs, out_specs=out_specs, out_shape=out_shape,
        scratch_shapes=scratch, name="mix_cd",
        input_output_aliases={n_in - 2: n_act_out, n_in - 1: n_act_out + 1},
        compiler_params=pltpu.CompilerParams(
            dimension_semantics=("arbitrary", "arbitrary"), vmem_limit_bytes=VMEM_LIMIT_BYTES),
    )(x, mab, st_hg, st_gla, p["norm_w"], p["w_b"], p["log_lb"], p["log1m_lb"], p["onem_lb"],
      p["hgrn_norm"], p["w_gk"], p["b_gk"], p["gk_scale"], p["gla_norm"], p["w_br_cd"], p["w_out"],
      p["final_norm"], jnp.asarray(_att_source(chunk, seg)), hg_buf, gl_buf)


def _pad_gla_heads(w):
    lead = w.shape[:-1]
    w4 = w.reshape(lead + (HEADS, GLA_DK))
    w4 = jnp.pad(w4, [(0, 0)] * len(lead) + [(0, 0), (0, HEAD_DIM - GLA_DK)])
    return w4.reshape(lead + (HEADS * HEAD_DIM,))


def _prep_params(norm_w, w_in, conv_w, s5_a_re, s5_a_im, s5_log_dt, s5_b_re, s5_b_im, s5_c_re,
                 s5_c_im, s5_d, w_glu, b_glu, hgrn_lb_raw, hgrn_norm, w_gk, b_gk, gla_norm,
                 w_branch, w_out, final_norm):
    f32 = _F32
    p = {}
    p["norm_w"] = norm_w.astype(f32)[:, None, :]
    p["w_a"] = jnp.concatenate(
        [w_in[:, :, 0:_OFF_CQ], w_in[:, :, _OFF_GATE:_OFF_GATE + 2 * D_MODEL]], axis=2).astype(_BF)
    p["w_b"] = jnp.concatenate(
        [w_in[:, :, _OFF_CQ:_OFF_DQ],
         _pad_gla_heads(w_in[:, :, _OFF_DQ:_OFF_DK]),
         _pad_gla_heads(w_in[:, :, _OFF_DK:_OFF_DV]),
         w_in[:, :, _OFF_DV:_OFF_DR],
         jnp.pad(w_in[:, :, _OFF_DR:_OFF_GATE], ((0, 0), (0, 0), (0, LANES - GLA_RANK))),
         w_in[:, :, _OFF_GATE + 2 * D_MODEL:_OFF_GATE + 4 * D_MODEL]], axis=2).astype(_BF)
    p["conv_w"] = conv_w.astype(f32)

    ar = s5_a_re.astype(f32)
    ai = s5_a_im.astype(f32)
    dt = jnp.exp(s5_log_dt.astype(f32))[..., None]
    mag = jnp.exp(dt * ar)
    abar_r = mag * jnp.cos(dt * ai)
    abar_i = mag * jnp.sin(dt * ai)
    den = ar * ar + ai * ai
    zr = ((abar_r - 1.0) * ar + abar_i * ai) / den
    zi = (abar_i * ar - (abar_r - 1.0) * ai) / den
    b_re = s5_b_re.astype(f32)
    b_im = s5_b_im.astype(f32)
    bbar_r = zr[..., None] * b_re - zi[..., None] * b_im
    bbar_i = zr[..., None] * b_im + zi[..., None] * b_re
    eye = jnp.eye(S5_GROUP, dtype=f32)

    def bd_in(bb):
        x = bb.reshape(DEPTH, 2, S5_GROUP, S5_STATE, S5_GROUP)
        x = jnp.transpose(x, (0, 1, 2, 4, 3))[:, :, :, :, None, :]
        x = x * eye[None, None, :, None, :, None]
        return x.reshape(DEPTH, 2, 256, 1024)

    def bd_out(c):
        x = c.reshape(DEPTH, 2, S5_GROUP, S5_GROUP, S5_STATE)
        x = jnp.transpose(x, (0, 1, 2, 4, 3))[:, :, :, :, None, :]
        x = x * eye[None, None, :, None, :, None]
        return x.reshape(DEPTH, 2, 1024, 256)

    p["bdb"] = jnp.stack([bd_in(bbar_r), bd_in(bbar_i)], axis=2).astype(_BF)
    p["bdc"] = jnp.stack([bd_out(s5_c_re.astype(f32)), bd_out(-s5_c_im.astype(f32))],
                         axis=2).astype(_BF)
    p["a_r"] = abar_r.reshape(DEPTH, 1, S5_FLAT)
    p["a_i"] = abar_i.reshape(DEPTH, 1, S5_FLAT)
    p["s5_d"] = s5_d.astype(f32)[:, None, :]
    p["w_glu"] = w_glu.astype(_BF)
    p["b_glu"] = b_glu.astype(f32)[:, None, :]
    p["w_br_ab"] = w_branch[:, 0:2].astype(_BF)
    p["w_br_cd"] = w_branch[:, 2:4].astype(_BF)
    p["w_out"] = w_out.astype(_BF)

    lb_cum = jnp.cumsum(jax.nn.softmax(hgrn_lb_raw.astype(f32), axis=0), axis=0)
    lb = lb_cum - lb_cum[0:1]
    p["log_lb"] = jnp.log(lb)[:, None, :]
    p["log1m_lb"] = jnp.log1p(-lb)[:, None, :]
    p["onem_lb"] = (1.0 - lb)[:, None, :]
    p["hgrn_norm"] = hgrn_norm.astype(f32)[:, None, :]
    p["gla_norm"] = gla_norm.astype(f32)[:, None, :]
    p["w_gk"] = jnp.pad(_pad_gla_heads(w_gk), ((0, 0), (0, LANES - GLA_RANK), (0, 0))).astype(_BF)
    p["b_gk"] = _pad_gla_heads(b_gk.astype(f32))[:, None, :]
    p["gk_scale"] = _pad_gla_heads(jnp.full((1, HEADS * GLA_DK), _LOG2E / GLA_GATE_NORM, f32))
    p["final_norm"] = final_norm.astype(f32)[None, :]
    return p


def _trunk(x, st_conv, st_re, st_im, st_hg, st_gla, p, *, ab_cfg, cd_cfg):
    bsz = x.shape[0]
    new_conv, new_re, new_im = [], [], []
    y = None
    st_re = st_re.reshape(DEPTH, bsz, S5_FLAT)
    st_im = st_im.reshape(DEPTH, bsz, S5_FLAT)
    new_hg = jnp.zeros(st_hg.shape, _F32)
    new_gla = jnp.zeros(st_gla.shape, _F32)
    for layer in range(DEPTH):
        mab, cbuf, h_r, h_i = _mix_ab(x, st_conv, st_re, st_im, p, layer, **ab_cfg)
        last = layer == DEPTH - 1
        outs = _mix_cd(x, mab, st_hg, st_gla, new_hg, new_gla, p, layer, last=last, **cd_cfg)
        if last:
            x, y, new_hg, new_gla = outs
        else:
            x, new_hg, new_gla = outs
        new_conv.append(cbuf)
        new_re.append(h_r.reshape(bsz, S5_GROUPS, S5_STATE))
        new_im.append(h_i.reshape(bsz, S5_GROUPS, S5_STATE))
    return (y, jnp.stack(new_conv, 0), jnp.stack(new_re, 0), jnp.stack(new_im, 0), new_hg, new_gla)


_PROMPT_AB = dict(tb=8, tl=64)
_PROMPT_CD = dict(tb=1, tl=256, chunk=128, seg=128)
_SAMPLE_AB = dict(tb=32, tl=8)
_SAMPLE_CD = dict(tb=16, tl=8, chunk=128, seg=8)


@jax.jit
def _forward(x_prompt, x_sample, state_conv, state_ssm_re, state_ssm_im, state_hgrn, state_gla,
             *params):
    p = _prep_params(*params)
    bp = x_prompt.shape[0]
    dt = x_prompt.dtype
    z_conv = jnp.zeros((DEPTH, bp, 2, W_BR), dt)
    z_ssm = jnp.zeros((DEPTH, bp, S5_GROUPS, S5_STATE), dt)
    z_hg = jnp.zeros((DEPTH, bp, HEADS, HEAD_DIM, HEAD_DIM), dt)
    z_gla = jnp.zeros((DEPTH, bp, HEADS, GLA_DK, HEAD_DIM), dt)
    y_p, conv_p, re_p, im_p, hg_p, gla_p = _trunk(
        x_prompt, z_conv, z_ssm, z_ssm, z_hg, z_gla, p, ab_cfg=_PROMPT_AB, cd_cfg=_PROMPT_CD)
    y_s, conv_s, re_s, im_s, hg_s, gla_s = _trunk(
        x_sample, state_conv, state_ssm_re, state_ssm_im, state_hgrn, state_gla, p,
        ab_cfg=_SAMPLE_AB, cd_cfg=_SAMPLE_CD)
    return (y_p, y_s, conv_p, conv_s, re_p, re_s, im_p, im_s, hg_p, hg_s, gla_p, gla_s)


def kernel(x_prompt, x_sample, state_conv, state_ssm_re, state_ssm_im, state_hgrn, state_gla,
           norm_w, w_in, conv_w, s5_a_re, s5_a_im, s5_log_dt, s5_b_re, s5_b_im, s5_c_re, s5_c_im,
           s5_d, w_glu, b_glu, hgrn_lb_raw, hgrn_norm, w_gk, b_gk, gla_norm, w_branch, w_out,
           final_norm):
    return _forward(x_prompt, x_sample, state_conv, state_ssm_re, state_ssm_im, state_hgrn,
                    state_gla, norm_w, w_in, conv_w, s5_a_re, s5_a_im, s5_log_dt, s5_b_re, s5_b_im,
                    s5_c_re, s5_c_im, s5_d, w_glu, b_glu, hgrn_lb_raw, hgrn_norm, w_gk, b_gk,
                    gla_norm, w_branch, w_out, final_norm)
```

```python
import functools
import math

import jax
import jax.numpy as jnp
import numpy as np
from jax import lax
from jax.experimental import pallas as pl
from jax.experimental.pallas import tpu as pltpu

D_MODEL = 1024
DEPTH = 4
W_BR = 512
S5_GROUPS = 32
S5_GROUP = 16
S5_STATE = 64
S5_FLAT = S5_GROUPS * S5_STATE
HEADS = 4
HEAD_DIM = 128
GLA_DK = 64
GLA_RANK = 16
GLA_GATE_NORM = 16.0
EPS = 1e-6
N_GEN_HEADS = 2 * HEADS
DIAG = 4
SUBLANES = 8
LANES = 128
BF16_ROWS = 16
VMEM_LIMIT_BYTES = 56 * 1024 * 1024

_OFF_CQ = 3072
_OFF_DQ = 5120
_OFF_DK = 5376
_OFF_DV = 5632
_OFF_DR = 6656
_OFF_GATE = 6672

_A_AX, _A_AB, _A_AC, _A_AZ, _A_SU, _A_SZ, _A_GA, _A_GB = 0, 512, 1024, 1536, 2048, 2560, 3072, 4096
_A_COLS = 5120
_B_CQ, _B_CF, _B_CI, _B_CZ, _B_DQ, _B_DK, _B_DV, _B_DZ, _B_DR, _B_GC, _B_GD = (
    0, 512, 1024, 1536, 2048, 2560, 3072, 3584, 4096, 4224, 5248)
_B_COLS = 6272
_G_CZ, _G_DZ, _G_GC, _G_GD = 0, 512, 1024, 2048
_G_COLS = 3072
_G_SLAB = 256
_PACK_ROWS = 128

_GELU_C = math.sqrt(2.0 / math.pi)
_LOG2E = math.log2(math.e)
_LN2 = math.log(2.0)
_BF = jnp.bfloat16
_F32 = jnp.float32


def _sigmoid(x):
    return 1.0 / (1.0 + jnp.exp(-x))


def _silu(x):
    return x * _sigmoid(x)


def _softplus_neg_abs(x):
    return jnp.log2(1.0 + jnp.exp2(jnp.abs(x) * (-_LOG2E))) * _LN2


def _log_sigmoid(x):
    return jnp.minimum(x, 0.0) - _softplus_neg_abs(x)


def _gelu_tanh(x):
    return 0.5 * x * (1.0 + jnp.tanh(_GELU_C * (x + 0.044715 * (x * x * x))))


def _rms_scale(x):
    return x * lax.rsqrt(jnp.mean(x * x, axis=-1, keepdims=True) + EPS)


def _dot(a, b):
    return jnp.dot(a, b, preferred_element_type=_F32)


def _mix_ab_kernel(x_ref, cs_ref, sr_ref, si_ref, nw_ref, w_ref, cw_ref, ar_ref, ai_ref, bdb_ref,
                   bdc_ref, d_ref, wglu_ref, bglu_ref, wbr_ref,
                   m_ref, cso_ref, sro_ref, sio_ref,
                   u_ref, bur_ref, bui_ref, y_ref, *, tb, tl):
    rows = tb * tl
    ngroup = tb // SUBLANES
    step = pl.program_id(1)

    @pl.when(step == 0)
    def _():
        cso_ref[...] = cs_ref[...]
        sro_ref[...] = sr_ref[...]
        sio_ref[...] = si_ref[...]

    x = x_ref[...].reshape(rows, D_MODEL)
    h = (_rms_scale(x) * nw_ref[...]).astype(_BF)

    def proj(c0, n):
        return _dot(h, w_ref[:, c0:c0 + n])

    v2 = proj(_A_AC, W_BR) * proj(_A_AX, W_BR)
    v3 = v2.reshape(tb, tl, W_BR)
    t_idx = lax.broadcasted_iota(jnp.int32, (tb, tl, W_BR), 1)
    buf = cso_ref[...]
    b0 = buf[:, 0:1, :]
    b1 = buf[:, 1:2, :]
    r1 = pltpu.roll(v2, 1, axis=0).reshape(tb, tl, W_BR)
    r2 = pltpu.roll(v2, 2, axis=0).reshape(tb, tl, W_BR)
    p1 = jnp.where(t_idx == 0, b1, r1)
    p2 = jnp.where(t_idx == 0, b0, jnp.where(t_idx == 1, b1, r2))
    cw = cw_ref[...]
    conv = p2 * cw[0:1, :] + p1 * cw[1:2, :] + v3 * cw[2:3, :]
    cso_ref[...] = v3[:, tl - 2:tl, :]
    y_a = proj(_A_AB, W_BR) * conv.reshape(rows, W_BR) * _silu(proj(_A_AZ, W_BR))

    u = proj(_A_SU, W_BR)
    nslab = W_BR // LANES
    for c in range(nslab):
        u_ref[c] = u[:, c * LANES:(c + 1) * LANES]
    up = jnp.concatenate(
        [jnp.concatenate(
            [u_ref[c, pl.ds(g * SUBLANES * tl + t, SUBLANES, stride=tl), :]
             for g in range(ngroup) for t in range(tl)], axis=0)
         for c in range(nslab)], axis=1).astype(_BF)
    for hh in range(2):
        uh = up[:, 256 * hh:256 * hh + 256]
        bur_ref[:, 1024 * hh:1024 * hh + 1024] = _dot(uh, bdb_ref[hh, 0])
        bui_ref[:, 1024 * hh:1024 * hh + 1024] = _dot(uh, bdb_ref[hh, 1])

    lane_chunk = 1024
    for g in range(ngroup):
        base = g * SUBLANES * tl
        srow = slice(g * SUBLANES, (g + 1) * SUBLANES)
        for lc in range(S5_FLAT // lane_chunk):
            ls = slice(lc * lane_chunk, (lc + 1) * lane_chunk)
            a_r = jnp.broadcast_to(ar_ref[:, ls], (SUBLANES, lane_chunk))
            a_i = jnp.broadcast_to(ai_ref[:, ls], (SUBLANES, lane_chunk))

            hr, hi = sro_ref[srow, ls], sio_ref[srow, ls]
            for t in range(tl):
                rs = slice(base + t * SUBLANES, base + (t + 1) * SUBLANES)
                hr, hi = (a_r * hr - a_i * hi + bur_ref[rs, ls],
                          a_r * hi + a_i * hr + bui_ref[rs, ls])
                bur_ref[rs, ls] = hr
                bui_ref[rs, ls] = hi
            sro_ref[srow, ls] = hr
            sio_ref[srow, ls] = hi

    for hh in range(2):
        cs = slice(1024 * hh, 1024 * hh + 1024)
        y_h = (_dot(bur_ref[:, cs].astype(_BF), bdc_ref[hh, 0])
               + _dot(bui_ref[:, cs].astype(_BF), bdc_ref[hh, 1]))
        y_ref[2 * hh] = y_h[:, 0:LANES]
        y_ref[2 * hh + 1] = y_h[:, LANES:2 * LANES]
    ys = jnp.concatenate(
        [jnp.concatenate(
            [y_ref[c, pl.ds(g * SUBLANES * tl + b8, tl, stride=SUBLANES), :]
             for g in range(ngroup) for b8 in range(SUBLANES)], axis=0)
         for c in range(nslab)], axis=1)
    y_s = ys + d_ref[...] * u
    sg = _gelu_tanh(y_s)
    glu = sg * _sigmoid(_dot(sg.astype(_BF), wglu_ref[...]) + bglu_ref[...])
    y_b = glu * _silu(proj(_A_SZ, W_BR))

    merged = (_sigmoid(proj(_A_GA, D_MODEL)) * _dot(y_a.astype(_BF), wbr_ref[0])
              + _sigmoid(proj(_A_GB, D_MODEL)) * _dot(y_b.astype(_BF), wbr_ref[1]))
    m_ref[...] = merged.reshape(tb, tl, D_MODEL)


def _mix_ab(x, st_conv, st_re, st_im, p, layer, *, tb, tl):
    bsz, seq, _ = x.shape
    rows = tb * tl
    grid = (bsz // tb, seq // tl)

    def lw(shape):
        nd = len(shape)
        return pl.BlockSpec((None,) + shape, lambda i, j, nd=nd: (layer,) + (0,) * nd,
                            pipeline_mode=pl.Buffered(1))

    in_specs = [
        pl.BlockSpec((tb, tl, D_MODEL), lambda i, j: (i, j, 0)),
        pl.BlockSpec((None, tb, 2, W_BR), lambda i, j: (layer, i, 0, 0)),
        pl.BlockSpec((None, tb, S5_FLAT), lambda i, j: (layer, i, 0)),
        pl.BlockSpec((None, tb, S5_FLAT), lambda i, j: (layer, i, 0)),
        lw((1, D_MODEL)), lw((D_MODEL, _A_COLS)), lw((3, W_BR)), lw((1, S5_FLAT)), lw((1, S5_FLAT)),
        lw((2, 2, 256, 1024)), lw((2, 2, 1024, 256)), lw((1, W_BR)), lw((W_BR, W_BR)),
        lw((1, W_BR)), lw((2, W_BR, D_MODEL)),
    ]
    out_specs = [
        pl.BlockSpec((tb, tl, D_MODEL), lambda i, j: (i, j, 0)),
        pl.BlockSpec((tb, 2, W_BR), lambda i, j: (i, 0, 0)),
        pl.BlockSpec((tb, S5_FLAT), lambda i, j: (i, 0)),
        pl.BlockSpec((tb, S5_FLAT), lambda i, j: (i, 0)),
    ]
    out_shape = [
        jax.ShapeDtypeStruct((bsz, seq, D_MODEL), _F32),
        jax.ShapeDtypeStruct((bsz, 2, W_BR), _F32),
        jax.ShapeDtypeStruct((bsz, S5_FLAT), _F32),
        jax.ShapeDtypeStruct((bsz, S5_FLAT), _F32),
    ]
    scratch = [
        pltpu.VMEM((W_BR // LANES, rows, LANES), _F32),
        pltpu.VMEM((rows, S5_FLAT), _F32),
        pltpu.VMEM((rows, S5_FLAT), _F32),
        pltpu.VMEM((W_BR // LANES, rows, LANES), _F32),
    ]
    return pl.pallas_call(
        functools.partial(_mix_ab_kernel, tb=tb, tl=tl),
        grid=grid, in_specs=in_specs, out_specs=out_specs, out_shape=out_shape,
        scratch_shapes=scratch, name="mix_ab",
        compiler_params=pltpu.CompilerParams(
            dimension_semantics=("arbitrary", "arbitrary"), vmem_limit_bytes=VMEM_LIMIT_BYTES),
    )(x, st_conv, st_re, st_im, p["norm_w"], p["w_a"], p["conv_w"], p["a_r"], p["a_i"], p["bdb"],
      p["bdc"], p["s5_d"], p["w_glu"], p["b_glu"], p["w_br_ab"])


def _att_levels(seg):
    levels = []
    m = DIAG
    while 2 * m <= seg:
        levels.append(m)
        m *= 2
    return levels


def _att_source(chunk, seg):
    ri, ci = np.indices((chunk, chunk))
    levels = _att_levels(seg)
    code = np.zeros((chunk, chunk), np.int32)
    for idx, m in enumerate(levels):
        same = (ri // (2 * m)) == (ci // (2 * m))
        code[same & (ri % (2 * m) >= m) & (ci % (2 * m) < m)] = 1 + idx
    for dd in range(DIAG):
        code[(ri - ci == dd) & (ri % DIAG >= dd)] = 1 + len(levels) + dd
    return code


def _mix_cd_kernel(x_ref, mab_ref, hg_ref, gl_ref, nw_ref, w_ref, loglb_ref, log1m_ref, onem_ref,
                   hgn_ref, wgk_ref, bgk_ref, gscale_ref, gln_ref, wbr_ref, wout_ref, fn_ref,
                   src_ref, hg_buf_ref, gl_buf_ref, *rest, tb, tl, chunk, seg, last):
    del hg_buf_ref, gl_buf_ref
    if last:
        xo_ref, yo_ref, hgo_ref, glo_ref = rest[:4]
        scr = rest[4:]
    else:
        xo_ref, hgo_ref, glo_ref = rest[:3]
        yo_ref = None
        scr = rest[3:]
    q_ref, k_ref, lf_ref, v_ref, g_ref, o_ref, h_ref = scr
    rows = tb * tl
    nchunk = rows // chunk
    nseg = chunk // seg
    step = pl.program_id(1)

    @pl.when(step == 0)
    def _():
        hgo_ref[...] = hg_ref[...]
        glo_ref[...] = gl_ref[...]

    h_ref[...] = (_rms_scale(x_ref[...].reshape(rows, D_MODEL)) * nw_ref[...]).astype(_BF)

    def proj(c0, n):
        return _dot(h_ref[...], w_ref[:, c0:c0 + n])

    def put(ref, lo, val):
        ref[:, lo:lo + val.shape[1]] = val

    def hgrn_gate(lo):
        def run():
            cs = slice(lo, lo + _G_SLAB)
            cf = proj(_B_CF + lo, _G_SLAB)
            gate_b = log1m_ref[:, cs] + _log_sigmoid(cf)
            gate_a = loglb_ref[:, cs]
            put(lf_ref, lo, (jnp.maximum(gate_a, gate_b) + _softplus_neg_abs(gate_a - gate_b)) * _LOG2E)
            put(k_ref, lo, onem_ref[:, cs] * _sigmoid(-cf))
        return run

    def gla_gate():
        gk_lin = _dot(proj(_B_DR, LANES).astype(_BF), wgk_ref[...]) + bgk_ref[...]
        put(lf_ref, W_BR, _log_sigmoid(gk_lin) * gscale_ref[...])

    def plain(ref, dst, src, fn):
        def run():
            put(ref, dst, fn(proj(src, _G_SLAB)))
        return run

    stage1 = []
    for lo in range(0, W_BR, _G_SLAB):
        stage1.append(hgrn_gate(lo))
        stage1.append(plain(q_ref, lo, _B_CQ + lo, lambda t: _silu(t) * (HEAD_DIM ** -0.5)))
        stage1.append(plain(v_ref, lo, _B_CI + lo, lambda t: t))
    stage1.append(gla_gate)
    for lo in range(0, W_BR, _G_SLAB):
        stage1.append(plain(q_ref, W_BR + lo, _B_DQ + lo, lambda t: t * (GLA_DK ** -0.5)))
        stage1.append(plain(k_ref, W_BR + lo, _B_DK + lo, lambda t: t))
        stage1.append(plain(v_ref, W_BR + lo, _B_DV + lo, lambda t: t))
    n_recurrence_inputs = len(stage1)
    for lo in range(0, W_BR, _G_SLAB):
        stage1.append(plain(g_ref, _G_CZ + lo, _B_CZ + lo, _silu))
        stage1.append(plain(g_ref, _G_DZ + lo, _B_DZ + lo, _silu))
    for lo in range(0, 2 * D_MODEL, _G_SLAB):
        stage1.append(plain(g_ref, _G_GC + lo, _B_GC + lo, _sigmoid))
    for _ in range(n_recurrence_inputs):
        stage1.pop(0)()

    nblk = chunk // SUBLANES
    blk_per_seg = seg // SUBLANES
    sub_i = lax.broadcasted_iota(jnp.int32, (SUBLANES, HEAD_DIM), 0)
    row_diag = lax.broadcasted_iota(jnp.int32, (chunk, HEAD_DIM), 0) & (DIAG - 1)
    eye = (lax.broadcasted_iota(jnp.int32, (HEAD_DIM, HEAD_DIM), 0)
           == lax.broadcasted_iota(jnp.int32, (HEAD_DIM, HEAD_DIM), 1))
    levels = _att_levels(seg)
    level_signs = []
    for m in levels:
        half = lax.broadcasted_iota(jnp.int32, (1, 2 * m, HEAD_DIM), 1) >= m
        level_signs.append(jnp.where(half, 1.0, -1.0).astype(_F32))

    def block_roll(a, shift):
        return jnp.concatenate(
            [pltpu.roll(a[j * SUBLANES:(j + 1) * SUBLANES, :], shift, axis=0)
             for j in range(nblk)], axis=0)

    def seg_cumsum(lf):
        out = []
        carry = None
        for j in range(nblk):
            blk = lf[j * SUBLANES:(j + 1) * SUBLANES, :]
            d = 1
            while d < SUBLANES:
                blk = blk + jnp.where(sub_i >= d, pltpu.roll(blk, d, axis=0), 0.0)
                d *= 2
            if j % blk_per_seg != 0:
                blk = blk + carry
            carry = blk[SUBLANES - 1:SUBLANES, :]
            out.append(blk)
        return jnp.concatenate(out, axis=0)

    def load_state(s, hd):
        if hd < HEADS:
            return hgo_ref[s, hd]
        return jnp.concatenate([glo_ref[s, hd - HEADS], jnp.zeros((GLA_DK, HEAD_DIM), _F32)], axis=0)

    def store_state(s, hd, val):
        if hd < HEADS:
            hgo_ref[s, hd] = val
        else:
            glo_ref[s, hd - HEADS] = val[0:GLA_DK, :]

    def prepare(c, hd):
        src = pl.ds(c * chunk, chunk)
        sl = slice(hd * HEAD_DIM, (hd + 1) * HEAD_DIM)
        q = q_ref[src, sl]
        k = k_ref[src, sl]
        v = v_ref[src, sl]
        b = seg_cumsum(lf_ref[src, sl])
        level_ops = []
        q_bf = q.astype(_BF)
        k_bf = k.astype(_BF)
        for m, sgn in zip(levels, level_signs):
            b3 = b.reshape(chunk // (2 * m), 2 * m, HEAD_DIM)
            dec = jnp.exp2((b3 - b3[:, m - 1:m, :]) * sgn).reshape(chunk, HEAD_DIM).astype(_BF)
            level_ops.append((q_bf * dec, k_bf * dec))
        diag_cols = []
        for dd in range(DIAG):
            if dd == 0:
                diag_cols.append(jnp.sum(q * k, axis=-1, keepdims=True))
            else:
                e = jnp.where(row_diag >= dd, b - block_roll(b, dd), 0.0)
                diag_cols.append(
                    jnp.sum(q * block_roll(k, dd) * jnp.exp2(e), axis=-1, keepdims=True))
        b_last = b.reshape(nseg, seg, HEAD_DIM)[:, seg - 1:seg, :]
        q_in = q * jnp.exp2(b)
        k_dec = k * jnp.exp2((b_last - b.reshape(nseg, seg, HEAD_DIM)).reshape(chunk, HEAD_DIM))
        dec_cols = [jnp.sum(jnp.where(eye, jnp.exp2(b[(s + 1) * seg - 1:(s + 1) * seg, :]), 0.0),
                            axis=1, keepdims=True) for s in range(nseg)]
        return level_ops, diag_cols, q_in, k_dec, v, dec_cols

    def finish(c, hd, prepared):
        level_ops, diag_cols, q_in, k_dec, v, dec_cols = prepared
        r0 = c * chunk
        sl = slice(hd * HEAD_DIM, (hd + 1) * HEAD_DIM)
        att = jnp.zeros((chunk, chunk), _F32)
        for idx, (q_m, k_m) in enumerate(level_ops):
            a_m = lax.dot_general(q_m, k_m, (((1,), (1,)), ((), ())), preferred_element_type=_F32)
            att = jnp.where(src_ref[...] == 1 + idx, a_m, att)
        for dd in range(DIAG):
            att = jnp.where(src_ref[...] == 1 + len(levels) + dd, diag_cols[dd], att)
        att = att.astype(_BF)

        def state_update(sidx, state, dec_col, kd_s, v_s):
            upd = lax.dot_general(kd_s, v_s, (((0,), (0,)), ((), ())), preferred_element_type=_F32)
            store_state(sidx, hd, state * dec_col + upd)

        if nseg == 1:
            state = load_state(0, hd)
            v_bf = v.astype(_BF)
            lhs = jnp.concatenate([q_in.astype(_BF), att], axis=1)
            rhs = jnp.concatenate([state.astype(_BF), v_bf], axis=0)
            o_ref[pl.ds(r0, chunk), sl] = _dot(lhs, rhs)
            state_update(0, state, dec_cols[0], k_dec.astype(_BF), v_bf)
        else:
            o_intra = _dot(att, v.astype(_BF))
            for s in range(nseg):
                rs = slice(s * seg, (s + 1) * seg)
                state = load_state(s, hd)
                q_s, kd_s, v_s = q_in[rs], k_dec[rs], v[rs]
                if seg < BF16_ROWS:
                    zpad = jnp.zeros((BF16_ROWS - seg, HEAD_DIM), _F32)
                    q_s = jnp.concatenate([q_s, zpad], axis=0)
                    kd_s = jnp.concatenate([kd_s, zpad], axis=0)
                    v_s = jnp.concatenate([v_s, zpad], axis=0)
                o_inter = _dot(q_s.astype(_BF), state.astype(_BF))[0:seg, :]
                o_ref[pl.ds(r0 + s * seg, seg), sl] = o_intra[rs] + o_inter
                state_update(s, state, dec_cols[s], kd_s.astype(_BF), v_s.astype(_BF))

    units = [(c, hd) for c in range(nchunk) for hd in range(N_GEN_HEADS)]
    prepared = prepare(*units[0])
    for n, unit in enumerate(units):
        following = prepare(*units[n + 1]) if n + 1 < len(units) else None
        finish(*unit, prepared)
        prepared = following
        if stage1:
            stage1.pop(0)()
    while stage1:
        stage1.pop(0)()


    def normed(lo, norm_row):
        parts = []
        for hd in range(lo, lo + HEADS):
            o_h = o_ref[:, hd * HEAD_DIM:(hd + 1) * HEAD_DIM]
            parts.append(_rms_scale(o_h) * norm_row)
        return jnp.concatenate(parts, axis=1)

    y_c = normed(0, hgn_ref[...]) * g_ref[:, _G_CZ:_G_CZ + W_BR]
    y_d = normed(HEADS, gln_ref[...]) * g_ref[:, _G_DZ:_G_DZ + W_BR]
    merged = (mab_ref[...].reshape(rows, D_MODEL)
              + g_ref[:, _G_GC:_G_GC + D_MODEL] * _dot(y_c.astype(_BF), wbr_ref[0])
              + g_ref[:, _G_GD:_G_GD + D_MODEL] * _dot(y_d.astype(_BF), wbr_ref[1]))
    x_new = x_ref[...].reshape(rows, D_MODEL) + _dot(merged.astype(_BF), wout_ref[...])
    xo_ref[...] = x_new.reshape(tb, tl, D_MODEL)
    if last:
        yo_ref[...] = (_rms_scale(x_new) * fn_ref[...]).reshape(tb, tl, D_MODEL)


def _mix_cd(x, mab, st_hg, st_gla, hg_buf, gl_buf, p, layer, *, tb, tl, chunk, seg, last):
    bsz, seq, _ = x.shape
    rows = tb * tl
    grid = (bsz // tb, seq // tl)

    def lw(shape):
        nd = len(shape)
        return pl.BlockSpec((None,) + shape, lambda i, j, nd=nd: (layer,) + (0,) * nd,
                            pipeline_mode=pl.Buffered(1))

    act = pl.BlockSpec((tb, tl, D_MODEL), lambda i, j: (i, j, 0))
    hg_block = (None, tb, HEADS, HEAD_DIM, HEAD_DIM)
    gl_block = (None, tb, HEADS, GLA_DK, HEAD_DIM)

    def state_map(i, j):
        return (layer, i, 0, 0, 0)

    any_spec = pl.BlockSpec(memory_space=pl.ANY)
    in_specs = [
        act, act,
        pl.BlockSpec(hg_block, state_map), pl.BlockSpec(gl_block, state_map),
        lw((1, D_MODEL)), lw((D_MODEL, _B_COLS)), lw((1, W_BR)), lw((1, W_BR)), lw((1, W_BR)),
        lw((1, HEAD_DIM)), lw((LANES, W_BR)), lw((1, W_BR)),
        pl.BlockSpec((1, W_BR), lambda i, j: (0, 0)),
        lw((1, HEAD_DIM)), lw((2, W_BR, D_MODEL)), lw((D_MODEL, D_MODEL)),
        pl.BlockSpec((1, D_MODEL), lambda i, j: (0, 0)),
        pl.BlockSpec((chunk, chunk), lambda i, j: (0, 0), pipeline_mode=pl.Buffered(1)),
        any_spec, any_spec,
    ]
    n_in = len(in_specs)
    act_shape = jax.ShapeDtypeStruct((bsz, seq, D_MODEL), _F32)
    n_act_out = 2 if last else 1
    out_specs = [act] * n_act_out + [pl.BlockSpec(hg_block, state_map),
                                     pl.BlockSpec(gl_block, state_map)]
    out_shape = [act_shape] * n_act_out + [
        jax.ShapeDtypeStruct(hg_buf.shape, _F32), jax.ShapeDtypeStruct(gl_buf.shape, _F32)]
    wide = N_GEN_HEADS * HEAD_DIM
    scratch = [pltpu.VMEM((rows, wide), _F32) for _ in range(4)] + [
        pltpu.VMEM((rows, _G_COLS), _F32),
        pltpu.VMEM((rows, wide), _F32),
        pltpu.VMEM((rows, D_MODEL), _BF)]
    return pl.pallas_call(
        functools.partial(_mix_cd_kernel, tb=tb, tl=tl, chunk=chunk, seg=seg, last=last),
        grid=grid, in_specs=in_specs, out_specs=out_specs, out_shape=out_shape,
        scratch_shapes=scratch, name="mix_cd",
        input_output_aliases={n_in - 2: n_act_out, n_in - 1: n_act_out + 1},
        compiler_params=pltpu.CompilerParams(
            dimension_semantics=("arbitrary", "arbitrary"), vmem_limit_bytes=VMEM_LIMIT_BYTES),
    )(x, mab, st_hg, st_gla, p["norm_w"], p["w_b"], p["log_lb"], p["log1m_lb"], p["onem_lb"],
      p["hgrn_norm"], p["w_gk"], p["b_gk"], p["gk_scale"], p["gla_norm"], p["w_br_cd"], p["w_out"],
      p["final_norm"], jnp.asarray(_att_source(chunk, seg)), hg_buf, gl_buf)


def _pad_gla_heads(w):
    lead = w.shape[:-1]
    w4 = w.reshape(lead + (HEADS, GLA_DK))
    w4 = jnp.pad(w4, [(0, 0)] * len(lead) + [(0, 0), (0, HEAD_DIM - GLA_DK)])
    return w4.reshape(lead + (HEADS * HEAD_DIM,))


def _pack_w_in_kernel(w_ref, wa_ref, wb_ref):
    rows = w_ref.shape[0]

    def cols(lo, n):
        return w_ref[:, lo:lo + n]

    def padded_heads(lo):
        zeros = jnp.zeros((rows, HEAD_DIM - GLA_DK), _F32)
        parts = []
        for hd in range(HEADS):
            parts += [cols(lo + hd * GLA_DK, GLA_DK), zeros]
        return jnp.concatenate(parts, axis=1)

    wa_ref[:, 0:_OFF_CQ] = cols(0, _OFF_CQ).astype(_BF)
    wa_ref[:, _A_GA:_A_COLS] = cols(_OFF_GATE, 2 * D_MODEL).astype(_BF)
    wb_ref[:, _B_CQ:_B_DQ] = cols(_OFF_CQ, _OFF_DQ - _OFF_CQ).astype(_BF)
    wb_ref[:, _B_DQ:_B_DK] = padded_heads(_OFF_DQ).astype(_BF)
    wb_ref[:, _B_DK:_B_DV] = padded_heads(_OFF_DK).astype(_BF)
    wb_ref[:, _B_DV:_B_DR] = cols(_OFF_DV, _OFF_DR - _OFF_DV).astype(_BF)
    wb_ref[:, _B_DR:_B_GC] = jnp.concatenate(
        [cols(_OFF_DR, GLA_RANK), jnp.zeros((rows, LANES - GLA_RANK), _F32)], axis=1).astype(_BF)
    wb_ref[:, _B_GC:_B_COLS] = cols(_OFF_GATE + 2 * D_MODEL, 2 * D_MODEL).astype(_BF)


def _pack_w_in(w_in):
    d_in = w_in.shape[-1]
    return pl.pallas_call(
        _pack_w_in_kernel,
        grid=(DEPTH, D_MODEL // _PACK_ROWS),
        in_specs=[pl.BlockSpec((None, _PACK_ROWS, d_in), lambda l, r: (l, r, 0))],
        out_specs=[pl.BlockSpec((None, _PACK_ROWS, _A_COLS), lambda l, r: (l, r, 0)),
                   pl.BlockSpec((None, _PACK_ROWS, _B_COLS), lambda l, r: (l, r, 0))],
        out_shape=[jax.ShapeDtypeStruct((DEPTH, D_MODEL, _A_COLS), _BF),
                   jax.ShapeDtypeStruct((DEPTH, D_MODEL, _B_COLS), _BF)],
        name="pack_w_in",
        compiler_params=pltpu.CompilerParams(
            dimension_semantics=("arbitrary", "arbitrary"), vmem_limit_bytes=VMEM_LIMIT_BYTES),
    )(w_in)


def _prep_params(norm_w, w_in, conv_w, s5_a_re, s5_a_im, s5_log_dt, s5_b_re, s5_b_im, s5_c_re,
                 s5_c_im, s5_d, w_glu, b_glu, hgrn_lb_raw, hgrn_norm, w_gk, b_gk, gla_norm,
                 w_branch, w_out, final_norm):
    f32 = _F32
    p = {}
    p["norm_w"] = norm_w.astype(f32)[:, None, :]
    p["w_a"], p["w_b"] = _pack_w_in(w_in)
    p["conv_w"] = conv_w.astype(f32)

    ar = s5_a_re.astype(f32)
    ai = s5_a_im.astype(f32)
    dt = jnp.exp(s5_log_dt.astype(f32))[..., None]
    mag = jnp.exp(dt * ar)
    abar_r = mag * jnp.cos(dt * ai)
    abar_i = mag * jnp.sin(dt * ai)
    den = ar * ar + ai * ai
    zr = ((abar_r - 1.0) * ar + abar_i * ai) / den
    zi = (abar_i * ar - (abar_r - 1.0) * ai) / den
    b_re = s5_b_re.astype(f32)
    b_im = s5_b_im.astype(f32)
    bbar_r = zr[..., None] * b_re - zi[..., None] * b_im
    bbar_i = zr[..., None] * b_im + zi[..., None] * b_re
    eye = jnp.eye(S5_GROUP, dtype=f32)

    def bd_in(bb):
        x = bb.reshape(DEPTH, 2, S5_GROUP, S5_STATE, S5_GROUP)
        x = jnp.transpose(x, (0, 1, 2, 4, 3))[:, :, :, :, None, :]
        x = x * eye[None, None, :, None, :, None]
        return x.reshape(DEPTH, 2, 256, 1024)

    def bd_out(c):
        x = c.reshape(DEPTH, 2, S5_GROUP, S5_GROUP, S5_STATE)
        x = jnp.transpose(x, (0, 1, 2, 4, 3))[:, :, :, :, None, :]
        x = x * eye[None, None, :, None, :, None]
        return x.reshape(DEPTH, 2, 1024, 256)

    p["bdb"] = jnp.stack([bd_in(bbar_r), bd_in(bbar_i)], axis=2).astype(_BF)
    p["bdc"] = jnp.stack([bd_out(s5_c_re.astype(f32)), bd_out(-s5_c_im.astype(f32))],
                         axis=2).astype(_BF)
    p["a_r"] = abar_r.reshape(DEPTH, 1, S5_FLAT)
    p["a_i"] = abar_i.reshape(DEPTH, 1, S5_FLAT)
    p["s5_d"] = s5_d.astype(f32)[:, None, :]
    p["w_glu"] = w_glu.astype(_BF)
    p["b_glu"] = b_glu.astype(f32)[:, None, :]
    p["w_br_ab"] = w_branch[:, 0:2].astype(_BF)
    p["w_br_cd"] = w_branch[:, 2:4].astype(_BF)
    p["w_out"] = w_out.astype(_BF)

    lb_cum = jnp.cumsum(jax.nn.softmax(hgrn_lb_raw.astype(f32), axis=0), axis=0)
    lb = lb_cum - lb_cum[0:1]
    p["log_lb"] = jnp.log(lb)[:, None, :]
    p["log1m_lb"] = jnp.log1p(-lb)[:, None, :]
    p["onem_lb"] = (1.0 - lb)[:, None, :]
    p["hgrn_norm"] = hgrn_norm.astype(f32)[:, None, :]
    p["gla_norm"] = gla_norm.astype(f32)[:, None, :]
    p["w_gk"] = jnp.pad(_pad_gla_heads(w_gk), ((0, 0), (0, LANES - GLA_RANK), (0, 0))).astype(_BF)
    p["b_gk"] = _pad_gla_heads(b_gk.astype(f32))[:, None, :]
    p["gk_scale"] = _pad_gla_heads(jnp.full((1, HEADS * GLA_DK), _LOG2E / GLA_GATE_NORM, f32))
    p["final_norm"] = final_norm.astype(f32)[None, :]
    return p


def _trunk(x, st_conv, st_re, st_im, st_hg, st_gla, p, *, ab_cfg, cd_cfg):
    bsz = x.shape[0]
    new_conv, new_re, new_im = [], [], []
    y = None
    st_re = st_re.reshape(DEPTH, bsz, S5_FLAT)
    st_im = st_im.reshape(DEPTH, bsz, S5_FLAT)
    new_hg = jnp.zeros(st_hg.shape, _F32)
    new_gla = jnp.zeros(st_gla.shape, _F32)
    for layer in range(DEPTH):
        mab, cbuf, h_r, h_i = _mix_ab(x, st_conv, st_re, st_im, p, layer, **ab_cfg)
        last = layer == DEPTH - 1
        outs = _mix_cd(x, mab, st_hg, st_gla, new_hg, new_gla, p, layer, last=last, **cd_cfg)
        if last:
            x, y, new_hg, new_gla = outs
        else:
            x, new_hg, new_gla = outs
        new_conv.append(cbuf)
        new_re.append(h_r.reshape(bsz, S5_GROUPS, S5_STATE))
        new_im.append(h_i.reshape(bsz, S5_GROUPS, S5_STATE))
    return (y, jnp.stack(new_conv, 0), jnp.stack(new_re, 0), jnp.stack(new_im, 0), new_hg, new_gla)


_PROMPT_AB = dict(tb=8, tl=64)
_PROMPT_CD = dict(tb=1, tl=256, chunk=128, seg=128)
_SAMPLE_AB = dict(tb=32, tl=8)
_SAMPLE_CD = dict(tb=16, tl=8, chunk=128, seg=8)


@jax.jit
def _forward(x_prompt, x_sample, state_conv, state_ssm_re, state_ssm_im, state_hgrn, state_gla,
             *params):
    p = _prep_params(*params)
    bp = x_prompt.shape[0]
    dt = x_prompt.dtype
    z_conv = jnp.zeros((DEPTH, bp, 2, W_BR), dt)
    z_ssm = jnp.zeros((DEPTH, bp, S5_GROUPS, S5_STATE), dt)
    z_hg = jnp.zeros((DEPTH, bp, HEADS, HEAD_DIM, HEAD_DIM), dt)
    z_gla = jnp.zeros((DEPTH, bp, HEADS, GLA_DK, HEAD_DIM), dt)
    y_p, conv_p, re_p, im_p, hg_p, gla_p = _trunk(
        x_prompt, z_conv, z_ssm, z_ssm, z_hg, z_gla, p, ab_cfg=_PROMPT_AB, cd_cfg=_PROMPT_CD)
    y_s, conv_s, re_s, im_s, hg_s, gla_s = _trunk(
        x_sample, state_conv, state_ssm_re, state_ssm_im, state_hgrn, state_gla, p,
        ab_cfg=_SAMPLE_AB, cd_cfg=_SAMPLE_CD)
    return (y_p, y_s, conv_p, conv_s, re_p, re_s, im_p, im_s, hg_p, hg_s, gla_p, gla_s)


def kernel(x_prompt, x_sample, state_conv, state_ssm_re, state_ssm_im, state_hgrn, state_gla,
           norm_w, w_in, conv_w, s5_a_re, s5_a_im, s5_log_dt, s5_b_re, s5_b_im, s5_c_re, s5_c_im,
           s5_d, w_glu, b_glu, hgrn_lb_raw, hgrn_norm, w_gk, b_gk, gla_norm, w_branch, w_out,
           final_norm):
    return _forward(x_prompt, x_sample, state_conv, state_ssm_re, state_ssm_im, state_hgrn,
                    state_gla, norm_w, w_in, conv_w, s5_a_re, s5_a_im, s5_log_dt, s5_b_re, s5_b_im,
                    s5_c_re, s5_c_im, s5_d, w_glu, b_glu, hgrn_lb_raw, hgrn_norm, w_gk, b_gk,
                    gla_norm, w_branch, w_out, final_norm)
```

```python
import functools
import math

import jax
import jax.numpy as jnp
import numpy as np
from jax import lax
from jax.experimental import pallas as pl
from jax.experimental.pallas import tpu as pltpu

D_MODEL = 1024
DEPTH = 4
W_BR = 512
S5_GROUPS = 32
S5_GROUP = 16
S5_STATE = 64
S5_FLAT = S5_GROUPS * S5_STATE
HEADS = 4
HEAD_DIM = 128
GLA_DK = 64
GLA_RANK = 16
GLA_GATE_NORM = 16.0
EPS = 1e-6
N_GEN_HEADS = 2 * HEADS
DIAG = 4
SUBLANES = 8
LANES = 128
BF16_ROWS = 16
VMEM_LIMIT_BYTES = 56 * 1024 * 1024

_OFF_CQ = 3072
_OFF_DQ = 5120
_OFF_DK = 5376
_OFF_DV = 5632
_OFF_DR = 6656
_OFF_GATE = 6672

_A_AX, _A_AB, _A_AC, _A_AZ, _A_SU, _A_SZ, _A_GA, _A_GB = 0, 512, 1024, 1536, 2048, 2560, 3072, 4096
_A_COLS = 5120
_B_CQ, _B_CF, _B_CI, _B_CZ, _B_DQ, _B_DK, _B_DV, _B_DZ, _B_DR, _B_GC, _B_GD = (
    0, 512, 1024, 1536, 2048, 2560, 3072, 3584, 4096, 4224, 5248)
_B_COLS = 6272
_G_CZ, _G_DZ, _G_GC, _G_GD = 0, 512, 1024, 2048
_G_COLS = 3072
_G_SLAB = 256
_PACK_ROWS = 128

_GELU_C = math.sqrt(2.0 / math.pi)
_LOG2E = math.log2(math.e)
_LN2 = math.log(2.0)
_BF = jnp.bfloat16
_F32 = jnp.float32


def _sigmoid(x):
    return 1.0 / (1.0 + jnp.exp(-x))


def _silu(x):
    return x * _sigmoid(x)


def _softplus_neg_abs(x):
    return jnp.log2(1.0 + jnp.exp2(jnp.abs(x) * (-_LOG2E))) * _LN2


def _log_sigmoid(x):
    return jnp.minimum(x, 0.0) - _softplus_neg_abs(x)


def _gelu_tanh(x):
    return 0.5 * x * (1.0 + jnp.tanh(_GELU_C * (x + 0.044715 * (x * x * x))))


def _rms_scale(x):
    return x * lax.rsqrt(jnp.mean(x * x, axis=-1, keepdims=True) + EPS)


def _dot(a, b):
    return jnp.dot(a, b, preferred_element_type=_F32)


def _mix_ab_kernel(x_ref, cs_ref, sr_ref, si_ref, nw_ref, w_ref, cw_ref, ar_ref, ai_ref, bdb_ref,
                   bdc_ref, d_ref, wglu_ref, bglu_ref, wbr_ref,
                   m_ref, cso_ref, sro_ref, sio_ref,
                   u_ref, bur_ref, bui_ref, y_ref, *, tb, tl):
    rows = tb * tl
    ngroup = tb // SUBLANES
    step = pl.program_id(1)

    @pl.when(step == 0)
    def _():
        cso_ref[...] = cs_ref[...]
        sro_ref[...] = sr_ref[...]
        sio_ref[...] = si_ref[...]

    x = x_ref[...].reshape(rows, D_MODEL)
    h = (_rms_scale(x) * nw_ref[...]).astype(_BF)

    def proj(c0, n):
        return _dot(h, w_ref[:, c0:c0 + n])

    v2 = proj(_A_AC, W_BR) * proj(_A_AX, W_BR)
    v3 = v2.reshape(tb, tl, W_BR)
    t_idx = lax.broadcasted_iota(jnp.int32, (tb, tl, W_BR), 1)
    buf = cso_ref[...]
    b0 = buf[:, 0:1, :]
    b1 = buf[:, 1:2, :]
    r1 = pltpu.roll(v2, 1, axis=0).reshape(tb, tl, W_BR)
    r2 = pltpu.roll(v2, 2, axis=0).reshape(tb, tl, W_BR)
    p1 = jnp.where(t_idx == 0, b1, r1)
    p2 = jnp.where(t_idx == 0, b0, jnp.where(t_idx == 1, b1, r2))
    cw = cw_ref[...]
    conv = p2 * cw[0:1, :] + p1 * cw[1:2, :] + v3 * cw[2:3, :]
    cso_ref[...] = v3[:, tl - 2:tl, :]
    y_a = proj(_A_AB, W_BR) * conv.reshape(rows, W_BR) * _silu(proj(_A_AZ, W_BR))

    u = proj(_A_SU, W_BR)
    nslab = W_BR // LANES
    for c in range(nslab):
        u_ref[c] = u[:, c * LANES:(c + 1) * LANES]
    up = jnp.concatenate(
        [jnp.concatenate(
            [u_ref[c, pl.ds(g * SUBLANES * tl + t, SUBLANES, stride=tl), :]
             for g in range(ngroup) for t in range(tl)], axis=0)
         for c in range(nslab)], axis=1).astype(_BF)
    for hh in range(2):
        uh = up[:, 256 * hh:256 * hh + 256]
        bur_ref[:, 1024 * hh:1024 * hh + 1024] = _dot(uh, bdb_ref[hh, 0])
        bui_ref[:, 1024 * hh:1024 * hh + 1024] = _dot(uh, bdb_ref[hh, 1])

    lane_chunk = 1024
    for g in range(ngroup):
        base = g * SUBLANES * tl
        srow = slice(g * SUBLANES, (g + 1) * SUBLANES)
        for lc in range(S5_FLAT // lane_chunk):
            ls = slice(lc * lane_chunk, (lc + 1) * lane_chunk)
            a_r = jnp.broadcast_to(ar_ref[:, ls], (SUBLANES, lane_chunk))
            a_i = jnp.broadcast_to(ai_ref[:, ls], (SUBLANES, lane_chunk))

            hr, hi = sro_ref[srow, ls], sio_ref[srow, ls]
            for t in range(tl):
                rs = slice(base + t * SUBLANES, base + (t + 1) * SUBLANES)
                hr, hi = (a_r * hr - a_i * hi + bur_ref[rs, ls],
                          a_r * hi + a_i * hr + bui_ref[rs, ls])
                bur_ref[rs, ls] = hr
                bui_ref[rs, ls] = hi
            sro_ref[srow, ls] = hr
            sio_ref[srow, ls] = hi

    for hh in range(2):
        cs = slice(1024 * hh, 1024 * hh + 1024)
        y_h = (_dot(bur_ref[:, cs].astype(_BF), bdc_ref[hh, 0])
               + _dot(bui_ref[:, cs].astype(_BF), bdc_ref[hh, 1]))
        y_ref[2 * hh] = y_h[:, 0:LANES]
        y_ref[2 * hh + 1] = y_h[:, LANES:2 * LANES]
    ys = jnp.concatenate(
        [jnp.concatenate(
            [y_ref[c, pl.ds(g * SUBLANES * tl + b8, tl, stride=SUBLANES), :]
             for g in range(ngroup) for b8 in range(SUBLANES)], axis=0)
         for c in range(nslab)], axis=1)
    y_s = ys + d_ref[...] * u
    sg = _gelu_tanh(y_s)
    glu = sg * _sigmoid(_dot(sg.astype(_BF), wglu_ref[...]) + bglu_ref[...])
    y_b = glu * _silu(proj(_A_SZ, W_BR))

    merged = (_sigmoid(proj(_A_GA, D_MODEL)) * _dot(y_a.astype(_BF), wbr_ref[0])
              + _sigmoid(proj(_A_GB, D_MODEL)) * _dot(y_b.astype(_BF), wbr_ref[1]))
    m_ref[...] = merged.reshape(tb, tl, D_MODEL)


def _mix_ab(x, st_conv, st_re, st_im, p, layer, *, tb, tl):
    bsz, seq, _ = x.shape
    rows = tb * tl
    grid = (bsz // tb, seq // tl)

    def lw(shape):
        nd = len(shape)
        return pl.BlockSpec((None,) + shape, lambda i, j, nd=nd: (layer,) + (0,) * nd,
                            pipeline_mode=pl.Buffered(1))

    in_specs = [
        pl.BlockSpec((tb, tl, D_MODEL), lambda i, j: (i, j, 0)),
        pl.BlockSpec((None, tb, 2, W_BR), lambda i, j: (layer, i, 0, 0)),
        pl.BlockSpec((None, tb, S5_FLAT), lambda i, j: (layer, i, 0)),
        pl.BlockSpec((None, tb, S5_FLAT), lambda i, j: (layer, i, 0)),
        lw((1, D_MODEL)), lw((D_MODEL, _A_COLS)), lw((3, W_BR)), lw((1, S5_FLAT)), lw((1, S5_FLAT)),
        lw((2, 2, 256, 1024)), lw((2, 2, 1024, 256)), lw((1, W_BR)), lw((W_BR, W_BR)),
        lw((1, W_BR)), lw((2, W_BR, D_MODEL)),
    ]
    out_specs = [
        pl.BlockSpec((tb, tl, D_MODEL), lambda i, j: (i, j, 0)),
        pl.BlockSpec((tb, 2, W_BR), lambda i, j: (i, 0, 0)),
        pl.BlockSpec((tb, S5_FLAT), lambda i, j: (i, 0)),
        pl.BlockSpec((tb, S5_FLAT), lambda i, j: (i, 0)),
    ]
    out_shape = [
        jax.ShapeDtypeStruct((bsz, seq, D_MODEL), _F32),
        jax.ShapeDtypeStruct((bsz, 2, W_BR), _F32),
        jax.ShapeDtypeStruct((bsz, S5_FLAT), _F32),
        jax.ShapeDtypeStruct((bsz, S5_FLAT), _F32),
    ]
    scratch = [
        pltpu.VMEM((W_BR // LANES, rows, LANES), _F32),
        pltpu.VMEM((rows, S5_FLAT), _F32),
        pltpu.VMEM((rows, S5_FLAT), _F32),
        pltpu.VMEM((W_BR // LANES, rows, LANES), _F32),
    ]
    return pl.pallas_call(
        functools.partial(_mix_ab_kernel, tb=tb, tl=tl),
        grid=grid, in_specs=in_specs, out_specs=out_specs, out_shape=out_shape,
        scratch_shapes=scratch, name="mix_ab",
        compiler_params=pltpu.CompilerParams(
            dimension_semantics=("arbitrary", "arbitrary"), vmem_limit_bytes=VMEM_LIMIT_BYTES),
    )(x, st_conv, st_re, st_im, p["norm_w"], p["w_a"], p["conv_w"], p["a_r"], p["a_i"], p["bdb"],
      p["bdc"], p["s5_d"], p["w_glu"], p["b_glu"], p["w_br_ab"])


def _att_levels(seg):
    levels = []
    m = DIAG
    while 2 * m <= seg:
        levels.append(m)
        m *= 2
    return levels


def _att_source(chunk, seg):
    ri, ci = np.indices((chunk, chunk))
    levels = _att_levels(seg)
    code = np.zeros((chunk, chunk), np.int32)
    for idx, m in enumerate(levels):
        same = (ri // (2 * m)) == (ci // (2 * m))
        code[same & (ri % (2 * m) >= m) & (ci % (2 * m) < m)] = 1 + idx
    for dd in range(DIAG):
        code[(ri - ci == dd) & (ri % DIAG >= dd)] = 1 + len(levels) + dd
    return code


def _mix_cd_kernel(x_ref, mab_ref, hg_ref, gl_ref, nw_ref, w_ref, loglb_ref, log1m_ref, onem_ref,
                   hgn_ref, wgk_ref, bgk_ref, gscale_ref, gln_ref, wbr_ref, wout_ref, fn_ref,
                   src_ref, hg_buf_ref, gl_buf_ref, *rest, tb, tl, chunk, seg, last):
    del hg_buf_ref, gl_buf_ref
    if last:
        xo_ref, yo_ref, hgo_ref, glo_ref = rest[:4]
        scr = rest[4:]
    else:
        xo_ref, hgo_ref, glo_ref = rest[:3]
        yo_ref = None
        scr = rest[3:]
    q_ref, k_ref, lf_ref, v_ref, g_ref, o_ref, h_ref = scr
    rows = tb * tl
    nchunk = rows // chunk
    nseg = chunk // seg
    step = pl.program_id(1)

    @pl.when(step == 0)
    def _():
        hgo_ref[...] = hg_ref[...]
        glo_ref[...] = gl_ref[...]

    h_ref[...] = (_rms_scale(x_ref[...].reshape(rows, D_MODEL)) * nw_ref[...]).astype(_BF)

    def proj(c0, n):
        return _dot(h_ref[...], w_ref[:, c0:c0 + n])

    def put(ref, lo, val):
        ref[:, lo:lo + val.shape[1]] = val

    def hgrn_gate(lo):
        def run():
            cs = slice(lo, lo + _G_SLAB)
            cf = proj(_B_CF + lo, _G_SLAB)
            gate_b = log1m_ref[:, cs] + _log_sigmoid(cf)
            gate_a = loglb_ref[:, cs]
            put(lf_ref, lo, (jnp.maximum(gate_a, gate_b) + _softplus_neg_abs(gate_a - gate_b)) * _LOG2E)
            put(k_ref, lo, onem_ref[:, cs] * _sigmoid(-cf))
        return run

    def gla_gate():
        gk_lin = _dot(proj(_B_DR, LANES).astype(_BF), wgk_ref[...]) + bgk_ref[...]
        put(lf_ref, W_BR, _log_sigmoid(gk_lin) * gscale_ref[...])

    def plain(ref, dst, src, fn):
        def run():
            put(ref, dst, fn(proj(src, _G_SLAB)))
        return run

    slabs = []
    for lo in range(0, W_BR, _G_SLAB):
        slabs.append(hgrn_gate(lo))
        slabs.append(plain(q_ref, lo, _B_CQ + lo, lambda t: _silu(t) * (HEAD_DIM ** -0.5)))
        slabs.append(plain(v_ref, lo, _B_CI + lo, lambda t: t))
    slabs.append(gla_gate)
    for lo in range(0, W_BR, _G_SLAB):
        slabs.append(plain(q_ref, W_BR + lo, _B_DQ + lo, lambda t: t * (GLA_DK ** -0.5)))
        slabs.append(plain(k_ref, W_BR + lo, _B_DK + lo, lambda t: t))
        slabs.append(plain(v_ref, W_BR + lo, _B_DV + lo, lambda t: t))
    n_recurrence_inputs = len(slabs)
    for lo in range(0, W_BR, _G_SLAB):
        slabs.append(plain(g_ref, _G_CZ + lo, _B_CZ + lo, _silu))
        slabs.append(plain(g_ref, _G_DZ + lo, _B_DZ + lo, _silu))
    for lo in range(0, 2 * D_MODEL, _G_SLAB):
        slabs.append(plain(g_ref, _G_GC + lo, _B_GC + lo, _sigmoid))
    for _ in range(n_recurrence_inputs):
        slabs.pop(0)()

    nblk = chunk // SUBLANES
    blk_per_seg = seg // SUBLANES
    sub_i = lax.broadcasted_iota(jnp.int32, (SUBLANES, HEAD_DIM), 0)
    row_diag = lax.broadcasted_iota(jnp.int32, (chunk, HEAD_DIM), 0) & (DIAG - 1)
    eye = (lax.broadcasted_iota(jnp.int32, (HEAD_DIM, HEAD_DIM), 0)
           == lax.broadcasted_iota(jnp.int32, (HEAD_DIM, HEAD_DIM), 1))
    levels = _att_levels(seg)
    level_signs = []
    for m in levels:
        half = lax.broadcasted_iota(jnp.int32, (1, 2 * m, HEAD_DIM), 1) >= m
        level_signs.append(jnp.where(half, 1.0, -1.0).astype(_F32))

    def block_roll(a, shift):
        return jnp.concatenate(
            [pltpu.roll(a[j * SUBLANES:(j + 1) * SUBLANES, :], shift, axis=0)
             for j in range(nblk)], axis=0)

    def seg_cumsum(lf):
        out = []
        carry = None
        for j in range(nblk):
            blk = lf[j * SUBLANES:(j + 1) * SUBLANES, :]
            d = 1
            while d < SUBLANES:
                blk = blk + jnp.where(sub_i >= d, pltpu.roll(blk, d, axis=0), 0.0)
                d *= 2
            if j % blk_per_seg != 0:
                blk = blk + carry
            carry = blk[SUBLANES - 1:SUBLANES, :]
            out.append(blk)
        return jnp.concatenate(out, axis=0)

    def load_state(s, hd):
        if hd < HEADS:
            return hgo_ref[s, hd]
        return jnp.concatenate([glo_ref[s, hd - HEADS], jnp.zeros((GLA_DK, HEAD_DIM), _F32)], axis=0)

    def store_state(s, hd, val):
        if hd < HEADS:
            hgo_ref[s, hd] = val
        else:
            glo_ref[s, hd - HEADS] = val[0:GLA_DK, :]

    def prepare(c, hd):
        src = pl.ds(c * chunk, chunk)
        sl = slice(hd * HEAD_DIM, (hd + 1) * HEAD_DIM)
        q = q_ref[src, sl]
        k = k_ref[src, sl]
        v = v_ref[src, sl]
        b = seg_cumsum(lf_ref[src, sl])
        level_ops = []
        q_bf = q.astype(_BF)
        k_bf = k.astype(_BF)
        for m, sgn in zip(levels, level_signs):
            b3 = b.reshape(chunk // (2 * m), 2 * m, HEAD_DIM)
            dec = jnp.exp2((b3 - b3[:, m - 1:m, :]) * sgn).reshape(chunk, HEAD_DIM).astype(_BF)
            level_ops.append((q_bf * dec, k_bf * dec))
        diag_cols = []
        for dd in range(DIAG):
            if dd == 0:
                diag_cols.append(jnp.sum(q * k, axis=-1, keepdims=True))
            else:
                e = jnp.where(row_diag >= dd, b - block_roll(b, dd), 0.0)
                diag_cols.append(
                    jnp.sum(q * block_roll(k, dd) * jnp.exp2(e), axis=-1, keepdims=True))
        b_last = b.reshape(nseg, seg, HEAD_DIM)[:, seg - 1:seg, :]
        q_in = q * jnp.exp2(b)
        k_dec = k * jnp.exp2((b_last - b.reshape(nseg, seg, HEAD_DIM)).reshape(chunk, HEAD_DIM))
        dec_cols = [jnp.sum(jnp.where(eye, jnp.exp2(b[(s + 1) * seg - 1:(s + 1) * seg, :]), 0.0),
                            axis=1, keepdims=True) for s in range(nseg)]
        return level_ops, diag_cols, q_in, k_dec, v, dec_cols

    def finish(c, hd, prepared):
        level_ops, diag_cols, q_in, k_dec, v, dec_cols = prepared
        r0 = c * chunk
        sl = slice(hd * HEAD_DIM, (hd + 1) * HEAD_DIM)
        att = jnp.zeros((chunk, chunk), _F32)
        for idx, (q_m, k_m) in enumerate(level_ops):
            a_m = lax.dot_general(q_m, k_m, (((1,), (1,)), ((), ())), preferred_element_type=_F32)
            att = jnp.where(src_ref[...] == 1 + idx, a_m, att)
        for dd in range(DIAG):
            att = jnp.where(src_ref[...] == 1 + len(levels) + dd, diag_cols[dd], att)
        att = att.astype(_BF)

        def state_update(sidx, state, dec_col, kd_s, v_s):
            upd = lax.dot_general(kd_s, v_s, (((0,), (0,)), ((), ())), preferred_element_type=_F32)
            store_state(sidx, hd, state * dec_col + upd)

        if nseg == 1:
            state = load_state(0, hd)
            v_bf = v.astype(_BF)
            lhs = jnp.concatenate([q_in.astype(_BF), att], axis=1)
            rhs = jnp.concatenate([state.astype(_BF), v_bf], axis=0)
            o_ref[pl.ds(r0, chunk), sl] = _dot(lhs, rhs)
            state_update(0, state, dec_cols[0], k_dec.astype(_BF), v_bf)
        else:
            o_intra = _dot(att, v.astype(_BF))
            for s in range(nseg):
                rs = slice(s * seg, (s + 1) * seg)
                state = load_state(s, hd)
                q_s, kd_s, v_s = q_in[rs], k_dec[rs], v[rs]
                if seg < BF16_ROWS:
                    zpad = jnp.zeros((BF16_ROWS - seg, HEAD_DIM), _F32)
                    q_s = jnp.concatenate([q_s, zpad], axis=0)
                    kd_s = jnp.concatenate([kd_s, zpad], axis=0)
                    v_s = jnp.concatenate([v_s, zpad], axis=0)
                o_inter = _dot(q_s.astype(_BF), state.astype(_BF))[0:seg, :]
                o_ref[pl.ds(r0 + s * seg, seg), sl] = o_intra[rs] + o_inter
                state_update(s, state, dec_cols[s], kd_s.astype(_BF), v_s.astype(_BF))

    units = [(c, hd) for c in range(nchunk) for hd in range(N_GEN_HEADS)]
    prepared = prepare(*units[0])
    for n, unit in enumerate(units):
        following = prepare(*units[n + 1]) if n + 1 < len(units) else None
        finish(*unit, prepared)
        prepared = following
        if slabs:
            slabs.pop(0)()
    while slabs:
        slabs.pop(0)()


    def normed(lo, norm_row):
        parts = []
        for hd in range(lo, lo + HEADS):
            o_h = o_ref[:, hd * HEAD_DIM:(hd + 1) * HEAD_DIM]
            parts.append(_rms_scale(o_h) * norm_row)
        return jnp.concatenate(parts, axis=1)

    y_c = normed(0, hgn_ref[...]) * g_ref[:, _G_CZ:_G_CZ + W_BR]
    y_d = normed(HEADS, gln_ref[...]) * g_ref[:, _G_DZ:_G_DZ + W_BR]
    merged = (mab_ref[...].reshape(rows, D_MODEL)
              + g_ref[:, _G_GC:_G_GC + D_MODEL] * _dot(y_c.astype(_BF), wbr_ref[0])
              + g_ref[:, _G_GD:_G_GD + D_MODEL] * _dot(y_d.astype(_BF), wbr_ref[1]))
    x_new = x_ref[...].reshape(rows, D_MODEL) + _dot(merged.astype(_BF), wout_ref[...])
    xo_ref[...] = x_new.reshape(tb, tl, D_MODEL)
    if last:
        yo_ref[...] = (_rms_scale(x_new) * fn_ref[...]).reshape(tb, tl, D_MODEL)


def _mix_cd(x, mab, st_hg, st_gla, hg_buf, gl_buf, p, layer, *, tb, tl, chunk, seg, last):
    bsz, seq, _ = x.shape
    rows = tb * tl
    grid = (bsz // tb, seq // tl)

    def lw(shape):
        nd = len(shape)
        return pl.BlockSpec((None,) + shape, lambda i, j, nd=nd: (layer,) + (0,) * nd,
                            pipeline_mode=pl.Buffered(1))

    act = pl.BlockSpec((tb, tl, D_MODEL), lambda i, j: (i, j, 0))
    hg_block = (None, tb, HEADS, HEAD_DIM, HEAD_DIM)
    gl_block = (None, tb, HEADS, GLA_DK, HEAD_DIM)

    def state_map(i, j):
        return (layer, i, 0, 0, 0)

    any_spec = pl.BlockSpec(memory_space=pl.ANY)
    in_specs = [
        act, act,
        pl.BlockSpec(hg_block, state_map), pl.BlockSpec(gl_block, state_map),
        lw((1, D_MODEL)), lw((D_MODEL, _B_COLS)), lw((1, W_BR)), lw((1, W_BR)), lw((1, W_BR)),
        lw((1, HEAD_DIM)), lw((LANES, W_BR)), lw((1, W_BR)),
        pl.BlockSpec((1, W_BR), lambda i, j: (0, 0)),
        lw((1, HEAD_DIM)), lw((2, W_BR, D_MODEL)), lw((D_MODEL, D_MODEL)),
        pl.BlockSpec((1, D_MODEL), lambda i, j: (0, 0)),
        pl.BlockSpec((chunk, chunk), lambda i, j: (0, 0), pipeline_mode=pl.Buffered(1)),
        any_spec, any_spec,
    ]
    n_in = len(in_specs)
    act_shape = jax.ShapeDtypeStruct((bsz, seq, D_MODEL), _F32)
    n_act_out = 2 if last else 1
    out_specs = [act] * n_act_out + [pl.BlockSpec(hg_block, state_map),
                                     pl.BlockSpec(gl_block, state_map)]
    out_shape = [act_shape] * n_act_out + [
        jax.ShapeDtypeStruct(hg_buf.shape, _F32), jax.ShapeDtypeStruct(gl_buf.shape, _F32)]
    wide = N_GEN_HEADS * HEAD_DIM
    scratch = [pltpu.VMEM((rows, wide), _F32) for _ in range(4)] + [
        pltpu.VMEM((rows, _G_COLS), _F32),
        pltpu.VMEM((rows, wide), _F32),
        pltpu.VMEM((rows, D_MODEL), _BF)]
    return pl.pallas_call(
        functools.partial(_mix_cd_kernel, tb=tb, tl=tl, chunk=chunk, seg=seg, last=last),
        grid=grid, in_specs=in_specs, out_specs=out_specs, out_shape=out_shape,
        scratch_shapes=scratch, name="mix_cd",
        input_output_aliases={n_in - 2: n_act_out, n_in - 1: n_act_out + 1},
        compiler_params=pltpu.CompilerParams(
            dimension_semantics=("arbitrary", "arbitrary"), vmem_limit_bytes=VMEM_LIMIT_BYTES),
    )(x, mab, st_hg, st_gla, p["norm_w"], p["w_b"], p["log_lb"], p["log1m_lb"], p["onem_lb"],
      p["hgrn_norm"], p["w_gk"], p["b_gk"], p["gk_scale"], p["gla_norm"], p["w_br_cd"], p["w_out"],
      p["final_norm"], jnp.asarray(_att_source(chunk, seg)), hg_buf, gl_buf)


def _pad_gla_heads(w):
    lead = w.shape[:-1]
    w4 = w.reshape(lead + (HEADS, GLA_DK))
    w4 = jnp.pad(w4, [(0, 0)] * len(lead) + [(0, 0), (0, HEAD_DIM - GLA_DK)])
    return w4.reshape(lead + (HEADS * HEAD_DIM,))


def _pack_w_in_kernel(w_ref, wa_ref, wb_ref):
    rows = w_ref.shape[0]

    def cols(lo, n):
        return w_ref[:, lo:lo + n]

    def padded_heads(lo):
        zeros = jnp.zeros((rows, HEAD_DIM - GLA_DK), _F32)
        parts = []
        for hd in range(HEADS):
            parts += [cols(lo + hd * GLA_DK, GLA_DK), zeros]
        return jnp.concatenate(parts, axis=1)

    wa_ref[:, 0:_OFF_CQ] = cols(0, _OFF_CQ).astype(_BF)
    wa_ref[:, _A_GA:_A_COLS] = cols(_OFF_GATE, 2 * D_MODEL).astype(_BF)
    wb_ref[:, _B_CQ:_B_DQ] = cols(_OFF_CQ, _OFF_DQ - _OFF_CQ).astype(_BF)
    wb_ref[:, _B_DQ:_B_DK] = padded_heads(_OFF_DQ).astype(_BF)
    wb_ref[:, _B_DK:_B_DV] = padded_heads(_OFF_DK).astype(_BF)
    wb_ref[:, _B_DV:_B_DR] = cols(_OFF_DV, _OFF_DR - _OFF_DV).astype(_BF)
    wb_ref[:, _B_DR:_B_GC] = jnp.concatenate(
        [cols(_OFF_DR, GLA_RANK), jnp.zeros((rows, LANES - GLA_RANK), _F32)], axis=1).astype(_BF)
    wb_ref[:, _B_GC:_B_COLS] = cols(_OFF_GATE + 2 * D_MODEL, 2 * D_MODEL).astype(_BF)


def _pack_w_in(w_in):
    d_in = w_in.shape[-1]
    return pl.pallas_call(
        _pack_w_in_kernel,
        grid=(DEPTH, D_MODEL // _PACK_ROWS),
        in_specs=[pl.BlockSpec((None, _PACK_ROWS, d_in), lambda l, r: (l, r, 0))],
        out_specs=[pl.BlockSpec((None, _PACK_ROWS, _A_COLS), lambda l, r: (l, r, 0)),
                   pl.BlockSpec((None, _PACK_ROWS, _B_COLS), lambda l, r: (l, r, 0))],
        out_shape=[jax.ShapeDtypeStruct((DEPTH, D_MODEL, _A_COLS), _BF),
                   jax.ShapeDtypeStruct((DEPTH, D_MODEL, _B_COLS), _BF)],
        name="pack_w_in",
        compiler_params=pltpu.CompilerParams(
            dimension_semantics=("arbitrary", "arbitrary"), vmem_limit_bytes=VMEM_LIMIT_BYTES),
    )(w_in)


def _prep_params(norm_w, w_in, conv_w, s5_a_re, s5_a_im, s5_log_dt, s5_b_re, s5_b_im, s5_c_re,
                 s5_c_im, s5_d, w_glu, b_glu, hgrn_lb_raw, hgrn_norm, w_gk, b_gk, gla_norm,
                 w_branch, w_out, final_norm):
    f32 = _F32
    p = {}
    p["norm_w"] = norm_w.astype(f32)[:, None, :]
    p["w_a"], p["w_b"] = _pack_w_in(w_in)
    p["conv_w"] = conv_w.astype(f32)

    ar = s5_a_re.astype(f32)
    ai = s5_a_im.astype(f32)
    dt = jnp.exp(s5_log_dt.astype(f32))[..., None]
    mag = jnp.exp(dt * ar)
    abar_r = mag * jnp.cos(dt * ai)
    abar_i = mag * jnp.sin(dt * ai)
    den = ar * ar + ai * ai
    zr = ((abar_r - 1.0) * ar + abar_i * ai) / den
    zi = (abar_i * ar - (abar_r - 1.0) * ai) / den
    b_re = s5_b_re.astype(f32)
    b_im = s5_b_im.astype(f32)
    bbar_r = zr[..., None] * b_re - zi[..., None] * b_im
    bbar_i = zr[..., None] * b_im + zi[..., None] * b_re
    eye = jnp.eye(S5_GROUP, dtype=f32)

    def bd_in(bb):
        x = bb.reshape(DEPTH, 2, S5_GROUP, S5_STATE, S5_GROUP)
        x = jnp.transpose(x, (0, 1, 2, 4, 3))[:, :, :, :, None, :]
        x = x * eye[None, None, :, None, :, None]
        return x.reshape(DEPTH, 2, 256, 1024)

    def bd_out(c):
        x = c.reshape(DEPTH, 2, S5_GROUP, S5_GROUP, S5_STATE)
        x = jnp.transpose(x, (0, 1, 2, 4, 3))[:, :, :, :, None, :]
        x = x * eye[None, None, :, None, :, None]
        return x.reshape(DEPTH, 2, 1024, 256)

    p["bdb"] = jnp.stack([bd_in(bbar_r), bd_in(bbar_i)], axis=2).astype(_BF)
    p["bdc"] = jnp.stack([bd_out(s5_c_re.astype(f32)), bd_out(-s5_c_im.astype(f32))],
                         axis=2).astype(_BF)
    p["a_r"] = abar_r.reshape(DEPTH, 1, S5_FLAT)
    p["a_i"] = abar_i.reshape(DEPTH, 1, S5_FLAT)
    p["s5_d"] = s5_d.astype(f32)[:, None, :]
    p["w_glu"] = w_glu.astype(_BF)
    p["b_glu"] = b_glu.astype(f32)[:, None, :]
    p["w_br_ab"] = w_branch[:, 0:2].astype(_BF)
    p["w_br_cd"] = w_branch[:, 2:4].astype(_BF)
    p["w_out"] = w_out.astype(_BF)

    lb_cum = jnp.cumsum(jax.nn.softmax(hgrn_lb_raw.astype(f32), axis=0), axis=0)
    lb = lb_cum - lb_cum[0:1]
    p["log_lb"] = jnp.log(lb)[:, None, :]
    p["log1m_lb"] = jnp.log1p(-lb)[:, None, :]
    p["onem_lb"] = (1.0 - lb)[:, None, :]
    p["hgrn_norm"] = hgrn_norm.astype(f32)[:, None, :]
    p["gla_norm"] = gla_norm.astype(f32)[:, None, :]
    p["w_gk"] = jnp.pad(_pad_gla_heads(w_gk), ((0, 0), (0, LANES - GLA_RANK), (0, 0))).astype(_BF)
    p["b_gk"] = _pad_gla_heads(b_gk.astype(f32))[:, None, :]
    p["gk_scale"] = _pad_gla_heads(jnp.full((1, HEADS * GLA_DK), _LOG2E / GLA_GATE_NORM, f32))
    p["final_norm"] = final_norm.astype(f32)[None, :]
    return p


def _trunk(x, st_conv, st_re, st_im, st_hg, st_gla, p, *, ab_cfg, cd_cfg):
    bsz = x.shape[0]
    new_conv, new_re, new_im = [], [], []
    y = None
    st_re = st_re.reshape(DEPTH, bsz, S5_FLAT)
    st_im = st_im.reshape(DEPTH, bsz, S5_FLAT)
    new_hg = jnp.zeros(st_hg.shape, _F32)
    new_gla = jnp.zeros(st_gla.shape, _F32)
    for layer in range(DEPTH):
        mab, cbuf, h_r, h_i = _mix_ab(x, st_conv, st_re, st_im, p, layer, **ab_cfg)
        last = layer == DEPTH - 1
        outs = _mix_cd(x, mab, st_hg, st_gla, new_hg, new_gla, p, layer, last=last, **cd_cfg)
        if last:
            x, y, new_hg, new_gla = outs
        else:
            x, new_hg, new_gla = outs
        new_conv.append(cbuf)
        new_re.append(h_r.reshape(bsz, S5_GROUPS, S5_STATE))
        new_im.append(h_i.reshape(bsz, S5_GROUPS, S5_STATE))
    return (y, jnp.stack(new_conv, 0), jnp.stack(new_re, 0), jnp.stack(new_im, 0), new_hg, new_gla)


_PROMPT_AB = dict(tb=8, tl=64)
_PROMPT_CD = dict(tb=1, tl=256, chunk=128, seg=128)
_SAMPLE_AB = dict(tb=32, tl=8)
_SAMPLE_CD = dict(tb=16, tl=8, chunk=128, seg=8)


@jax.jit
def _forward(x_prompt, x_sample, state_conv, state_ssm_re, state_ssm_im, state_hgrn, state_gla,
             *params):
    p = _prep_params(*params)
    bp = x_prompt.shape[0]
    dt = x_prompt.dtype
    z_conv = jnp.zeros((DEPTH, bp, 2, W_BR), dt)
    z_ssm = jnp.zeros((DEPTH, bp, S5_GROUPS, S5_STATE), dt)
    z_hg = jnp.zeros((DEPTH, bp, HEADS, HEAD_DIM, HEAD_DIM), dt)
    z_gla = jnp.zeros((DEPTH, bp, HEADS, GLA_DK, HEAD_DIM), dt)
    y_p, conv_p, re_p, im_p, hg_p, gla_p = _trunk(
        x_prompt, z_conv, z_ssm, z_ssm, z_hg, z_gla, p, ab_cfg=_PROMPT_AB, cd_cfg=_PROMPT_CD)
    y_s, conv_s, re_s, im_s, hg_s, gla_s = _trunk(
        x_sample, state_conv, state_ssm_re, state_ssm_im, state_hgrn, state_gla, p,
        ab_cfg=_SAMPLE_AB, cd_cfg=_SAMPLE_CD)
    return (y_p, y_s, conv_p, conv_s, re_p, re_s, im_p, im_s, hg_p, hg_s, gla_p, gla_s)


def kernel(x_prompt, x_sample, state_conv, state_ssm_re, state_ssm_im, state_hgrn, state_gla,
           norm_w, w_in, conv_w, s5_a_re, s5_a_im, s5_log_dt, s5_b_re, s5_b_im, s5_c_re, s5_c_im,
           s5_d, w_glu, b_glu, hgrn_lb_raw, hgrn_norm, w_gk, b_gk, gla_norm, w_branch, w_out,
           final_norm):
    return _forward(x_prompt, x_sample, state_conv, state_ssm_re, state_ssm_im, state_hgrn,
                    state_gla, norm_w, w_in, conv_w, s5_a_re, s5_a_im, s5_log_dt, s5_b_re, s5_b_im,
                    s5_c_re, s5_c_im, s5_d, w_glu, b_glu, hgrn_lb_raw, hgrn_norm, w_gk, b_gk,
                    gla_norm, w_branch, w_out, final_norm)
```

```python
import functools
import math

import jax
import jax.numpy as jnp
import numpy as np
from jax import lax
from jax.experimental import pallas as pl
from jax.experimental.pallas import tpu as pltpu

D_MODEL = 1024
DEPTH = 4
W_BR = 512
S5_GROUPS = 32
S5_GROUP = 16
S5_STATE = 64
S5_FLAT = S5_GROUPS * S5_STATE
HEADS = 4
HEAD_DIM = 128
GLA_DK = 64
GLA_RANK = 16
GLA_GATE_NORM = 16.0
EPS = 1e-6
N_GEN_HEADS = 2 * HEADS
DIAG = 4
SUBLANES = 8
LANES = 128
BF16_ROWS = 16
VMEM_LIMIT_BYTES = 56 * 1024 * 1024

_OFF_CQ = 3072
_OFF_DQ = 5120
_OFF_DK = 5376
_OFF_DV = 5632
_OFF_DR = 6656
_OFF_GATE = 6672

_A_AX, _A_AB, _A_AC, _A_AZ, _A_SU, _A_SZ, _A_GA, _A_GB = 0, 512, 1024, 1536, 2048, 2560, 3072, 4096
_A_COLS = 5120
_B_CQ, _B_CF, _B_CI, _B_CZ, _B_DQ, _B_DK, _B_DV, _B_DZ, _B_DR, _B_GC, _B_GD = (
    0, 512, 1024, 1536, 2048, 2560, 3072, 3584, 4096, 4224, 5248)
_B_COLS = 6272
_G_CZ, _G_DZ, _G_GC, _G_GD = 0, 512, 1024, 2048
_G_COLS = 3072
_G_SLAB = 256
_PACK_ROWS = 128

_GELU_C = math.sqrt(2.0 / math.pi)
_LOG2E = math.log2(math.e)
_LN2 = math.log(2.0)
_BF = jnp.bfloat16
_F32 = jnp.float32


def _sigmoid(x):
    return 1.0 / (1.0 + jnp.exp(-x))


def _silu(x):
    return x * _sigmoid(x)


def _softplus_neg_abs(x):
    return jnp.log2(1.0 + jnp.exp2(jnp.abs(x) * (-_LOG2E))) * _LN2


def _log_sigmoid(x):
    return jnp.minimum(x, 0.0) - _softplus_neg_abs(x)


def _gelu_tanh(x):
    return 0.5 * x * (1.0 + jnp.tanh(_GELU_C * (x + 0.044715 * (x * x * x))))


def _rms_scale(x):
    return x * lax.rsqrt(jnp.mean(x * x, axis=-1, keepdims=True) + EPS)


def _dot(a, b):
    return jnp.dot(a, b, preferred_element_type=_F32)


def _mix_ab_kernel(x_ref, cs_ref, sr_ref, si_ref, nw_ref, w_ref, cw_ref, ar_ref, ai_ref, bdb_ref,
                   bdc_ref, d_ref, wglu_ref, bglu_ref, wbr_ref,
                   m_ref, cso_ref, sro_ref, sio_ref,
                   u_ref, bur_ref, bui_ref, y_ref, g_ref, ya_ref, *, tb, tl):
    rows = tb * tl
    ngroup = tb // SUBLANES
    step = pl.program_id(1)

    @pl.when(step == 0)
    def _():
        cso_ref[...] = cs_ref[...]
        sro_ref[...] = sr_ref[...]
        sio_ref[...] = si_ref[...]

    x = x_ref[...].reshape(rows, D_MODEL)
    h = (_rms_scale(x) * nw_ref[...]).astype(_BF)

    def proj(c0, n):
        return _dot(h, w_ref[:, c0:c0 + n])

    def conv_branch():
        v2 = proj(_A_AC, W_BR) * proj(_A_AX, W_BR)
        v3 = v2.reshape(tb, tl, W_BR)
        t_idx = lax.broadcasted_iota(jnp.int32, (tb, tl, W_BR), 1)
        buf = cso_ref[...]
        b0 = buf[:, 0:1, :]
        b1 = buf[:, 1:2, :]
        r1 = pltpu.roll(v2, 1, axis=0).reshape(tb, tl, W_BR)
        r2 = pltpu.roll(v2, 2, axis=0).reshape(tb, tl, W_BR)
        p1 = jnp.where(t_idx == 0, b1, r1)
        p2 = jnp.where(t_idx == 0, b0, jnp.where(t_idx == 1, b1, r2))
        cw = cw_ref[...]
        conv = p2 * cw[0:1, :] + p1 * cw[1:2, :] + v3 * cw[2:3, :]
        cso_ref[...] = v3[:, tl - 2:tl, :]
        ya_ref[...] = proj(_A_AB, W_BR) * conv.reshape(rows, W_BR) * _silu(proj(_A_AZ, W_BR))

    u = proj(_A_SU, W_BR)
    nslab = W_BR // LANES
    for c in range(nslab):
        u_ref[c] = u[:, c * LANES:(c + 1) * LANES]
    up = jnp.concatenate(
        [jnp.concatenate(
            [u_ref[c, pl.ds(g * SUBLANES * tl + t, SUBLANES, stride=tl), :]
             for g in range(ngroup) for t in range(tl)], axis=0)
         for c in range(nslab)], axis=1).astype(_BF)
    for hh in range(2):
        uh = up[:, 256 * hh:256 * hh + 256]
        bur_ref[:, 1024 * hh:1024 * hh + 1024] = _dot(uh, bdb_ref[hh, 0])
        bui_ref[:, 1024 * hh:1024 * hh + 1024] = _dot(uh, bdb_ref[hh, 1])

    def gate_slab(lo):
        def run():
            g_ref[:, lo:lo + _G_SLAB] = _sigmoid(proj(_A_GA + lo, _G_SLAB))
        return run

    gate_slabs = [gate_slab(lo) for lo in range(0, 2 * D_MODEL, _G_SLAB)]
    gate_slabs.insert(2, conv_branch)
    lane_chunk = 1024
    n_scan_steps = ngroup * (S5_FLAT // lane_chunk) * tl
    steps_per_slab = max(1, n_scan_steps // len(gate_slabs))
    n_done = 0
    for g in range(ngroup):
        base = g * SUBLANES * tl
        srow = slice(g * SUBLANES, (g + 1) * SUBLANES)
        for lc in range(S5_FLAT // lane_chunk):
            ls = slice(lc * lane_chunk, (lc + 1) * lane_chunk)
            a_r = jnp.broadcast_to(ar_ref[:, ls], (SUBLANES, lane_chunk))
            a_i = jnp.broadcast_to(ai_ref[:, ls], (SUBLANES, lane_chunk))

            hr, hi = sro_ref[srow, ls], sio_ref[srow, ls]
            for t in range(tl):
                rs = slice(base + t * SUBLANES, base + (t + 1) * SUBLANES)
                hr, hi = (a_r * hr - a_i * hi + bur_ref[rs, ls],
                          a_r * hi + a_i * hr + bui_ref[rs, ls])
                bur_ref[rs, ls] = hr
                bui_ref[rs, ls] = hi
                n_done += 1
                if gate_slabs and n_done % steps_per_slab == 0:
                    gate_slabs.pop(0)()
            sro_ref[srow, ls] = hr
            sio_ref[srow, ls] = hi
    while gate_slabs:
        gate_slabs.pop(0)()

    for hh in range(2):
        cs = slice(1024 * hh, 1024 * hh + 1024)
        y_h = (_dot(bur_ref[:, cs].astype(_BF), bdc_ref[hh, 0])
               + _dot(bui_ref[:, cs].astype(_BF), bdc_ref[hh, 1]))
        y_ref[2 * hh] = y_h[:, 0:LANES]
        y_ref[2 * hh + 1] = y_h[:, LANES:2 * LANES]
    ys = jnp.concatenate(
        [jnp.concatenate(
            [y_ref[c, pl.ds(g * SUBLANES * tl + b8, tl, stride=SUBLANES), :]
             for g in range(ngroup) for b8 in range(SUBLANES)], axis=0)
         for c in range(nslab)], axis=1)
    y_s = ys + d_ref[...] * u
    sg = _gelu_tanh(y_s)
    glu = sg * _sigmoid(_dot(sg.astype(_BF), wglu_ref[...]) + bglu_ref[...])
    y_b = glu * _silu(proj(_A_SZ, W_BR))

    merged = (g_ref[:, 0:D_MODEL] * _dot(ya_ref[...].astype(_BF), wbr_ref[0])
              + g_ref[:, D_MODEL:2 * D_MODEL] * _dot(y_b.astype(_BF), wbr_ref[1]))
    m_ref[...] = merged.reshape(tb, tl, D_MODEL)


def _mix_ab(x, st_conv, st_re, st_im, p, layer, *, tb, tl):
    bsz, seq, _ = x.shape
    rows = tb * tl
    grid = (bsz // tb, seq // tl)

    def lw(shape):
        nd = len(shape)
        return pl.BlockSpec((None,) + shape, lambda i, j, nd=nd: (layer,) + (0,) * nd,
                            pipeline_mode=pl.Buffered(1))

    in_specs = [
        pl.BlockSpec((tb, tl, D_MODEL), lambda i, j: (i, j, 0)),
        pl.BlockSpec((None, tb, 2, W_BR), lambda i, j: (layer, i, 0, 0)),
        pl.BlockSpec((None, tb, S5_FLAT), lambda i, j: (layer, i, 0)),
        pl.BlockSpec((None, tb, S5_FLAT), lambda i, j: (layer, i, 0)),
        lw((1, D_MODEL)), lw((D_MODEL, _A_COLS)), lw((3, W_BR)), lw((1, S5_FLAT)), lw((1, S5_FLAT)),
        lw((2, 2, 256, 1024)), lw((2, 2, 1024, 256)), lw((1, W_BR)), lw((W_BR, W_BR)),
        lw((1, W_BR)), lw((2, W_BR, D_MODEL)),
    ]
    out_specs = [
        pl.BlockSpec((tb, tl, D_MODEL), lambda i, j: (i, j, 0)),
        pl.BlockSpec((tb, 2, W_BR), lambda i, j: (i, 0, 0)),
        pl.BlockSpec((tb, S5_FLAT), lambda i, j: (i, 0)),
        pl.BlockSpec((tb, S5_FLAT), lambda i, j: (i, 0)),
    ]
    out_shape = [
        jax.ShapeDtypeStruct((bsz, seq, D_MODEL), _F32),
        jax.ShapeDtypeStruct((bsz, 2, W_BR), _F32),
        jax.ShapeDtypeStruct((bsz, S5_FLAT), _F32),
        jax.ShapeDtypeStruct((bsz, S5_FLAT), _F32),
    ]
    scratch = [
        pltpu.VMEM((W_BR // LANES, rows, LANES), _F32),
        pltpu.VMEM((rows, S5_FLAT), _F32),
        pltpu.VMEM((rows, S5_FLAT), _F32),
        pltpu.VMEM((W_BR // LANES, rows, LANES), _F32),
        pltpu.VMEM((rows, 2 * D_MODEL), _F32),
        pltpu.VMEM((rows, W_BR), _F32),
    ]
    return pl.pallas_call(
        functools.partial(_mix_ab_kernel, tb=tb, tl=tl),
        grid=grid, in_specs=in_specs, out_specs=out_specs, out_shape=out_shape,
        scratch_shapes=scratch, name="mix_ab",
        compiler_params=pltpu.CompilerParams(
            dimension_semantics=("arbitrary", "arbitrary"), vmem_limit_bytes=VMEM_LIMIT_BYTES),
    )(x, st_conv, st_re, st_im, p["norm_w"], p["w_a"], p["conv_w"], p["a_r"], p["a_i"], p["bdb"],
      p["bdc"], p["s5_d"], p["w_glu"], p["b_glu"], p["w_br_ab"])


def _att_levels(seg):
    levels = []
    m = DIAG
    while 2 * m <= seg:
        levels.append(m)
        m *= 2
    return levels


def _att_source(chunk, seg):
    ri, ci = np.indices((chunk, chunk))
    levels = _att_levels(seg)
    code = np.zeros((chunk, chunk), np.int32)
    for idx, m in enumerate(levels):
        same = (ri // (2 * m)) == (ci // (2 * m))
        code[same & (ri % (2 * m) >= m) & (ci % (2 * m) < m)] = 1 + idx
    for dd in range(DIAG):
        code[(ri - ci == dd) & (ri % DIAG >= dd)] = 1 + len(levels) + dd
    return code


def _mix_cd_kernel(x_ref, mab_ref, hg_ref, gl_ref, nw_ref, w_ref, loglb_ref, log1m_ref, onem_ref,
                   hgn_ref, wgk_ref, bgk_ref, gscale_ref, gln_ref, wbr_ref, wout_ref, fn_ref,
                   src_ref, hg_buf_ref, gl_buf_ref, *rest, tb, tl, chunk, seg, last):
    del hg_buf_ref, gl_buf_ref
    if last:
        xo_ref, yo_ref, hgo_ref, glo_ref = rest[:4]
        scr = rest[4:]
    else:
        xo_ref, hgo_ref, glo_ref = rest[:3]
        yo_ref = None
        scr = rest[3:]
    q_ref, k_ref, lf_ref, v_ref, g_ref, o_ref, h_ref = scr
    rows = tb * tl
    nchunk = rows // chunk
    nseg = chunk // seg
    step = pl.program_id(1)

    @pl.when(step == 0)
    def _():
        hgo_ref[...] = hg_ref[...]
        glo_ref[...] = gl_ref[...]

    h_ref[...] = (_rms_scale(x_ref[...].reshape(rows, D_MODEL)) * nw_ref[...]).astype(_BF)

    def proj(c0, n):
        return _dot(h_ref[...], w_ref[:, c0:c0 + n])

    def put(ref, lo, val):
        ref[:, lo:lo + val.shape[1]] = val

    def hgrn_gate(lo):
        def run():
            cs = slice(lo, lo + _G_SLAB)
            cf = proj(_B_CF + lo, _G_SLAB)
            gate_b = log1m_ref[:, cs] + _log_sigmoid(cf)
            gate_a = loglb_ref[:, cs]
            put(lf_ref, lo, (jnp.maximum(gate_a, gate_b) + _softplus_neg_abs(gate_a - gate_b)) * _LOG2E)
            put(k_ref, lo, onem_ref[:, cs] * _sigmoid(-cf))
        return run

    def gla_gate():
        gk_lin = _dot(proj(_B_DR, LANES).astype(_BF), wgk_ref[...]) + bgk_ref[...]
        put(lf_ref, W_BR, _log_sigmoid(gk_lin) * gscale_ref[...])

    def plain(ref, dst, src, fn):
        def run():
            put(ref, dst, fn(proj(src, _G_SLAB)))
        return run

    slabs = []
    for lo in range(0, W_BR, _G_SLAB):
        slabs.append(hgrn_gate(lo))
        slabs.append(plain(q_ref, lo, _B_CQ + lo, lambda t: _silu(t) * (HEAD_DIM ** -0.5)))
        slabs.append(plain(v_ref, lo, _B_CI + lo, lambda t: t))
    slabs.append(gla_gate)
    for lo in range(0, W_BR, _G_SLAB):
        slabs.append(plain(q_ref, W_BR + lo, _B_DQ + lo, lambda t: t * (GLA_DK ** -0.5)))
        slabs.append(plain(k_ref, W_BR + lo, _B_DK + lo, lambda t: t))
        slabs.append(plain(v_ref, W_BR + lo, _B_DV + lo, lambda t: t))
    n_recurrence_inputs = len(slabs)
    for lo in range(0, W_BR, _G_SLAB):
        slabs.append(plain(g_ref, _G_CZ + lo, _B_CZ + lo, _silu))
        slabs.append(plain(g_ref, _G_DZ + lo, _B_DZ + lo, _silu))
    for lo in range(0, 2 * D_MODEL, _G_SLAB):
        slabs.append(plain(g_ref, _G_GC + lo, _B_GC + lo, _sigmoid))
    for _ in range(n_recurrence_inputs):
        slabs.pop(0)()

    nblk = chunk // SUBLANES
    blk_per_seg = seg // SUBLANES
    sub_i = lax.broadcasted_iota(jnp.int32, (SUBLANES, HEAD_DIM), 0)
    row_diag = lax.broadcasted_iota(jnp.int32, (chunk, HEAD_DIM), 0) & (DIAG - 1)
    eye = (lax.broadcasted_iota(jnp.int32, (HEAD_DIM, HEAD_DIM), 0)
           == lax.broadcasted_iota(jnp.int32, (HEAD_DIM, HEAD_DIM), 1))
    levels = _att_levels(seg)
    level_signs = []
    for m in levels:
        half = lax.broadcasted_iota(jnp.int32, (1, 2 * m, HEAD_DIM), 1) >= m
        level_signs.append(jnp.where(half, 1.0, -1.0).astype(_F32))

    def block_roll(a, shift):
        return jnp.concatenate(
            [pltpu.roll(a[j * SUBLANES:(j + 1) * SUBLANES, :], shift, axis=0)
             for j in range(nblk)], axis=0)

    def seg_cumsum(lf):
        out = []
        carry = None
        for j in range(nblk):
            blk = lf[j * SUBLANES:(j + 1) * SUBLANES, :]
            d = 1
            while d < SUBLANES:
                blk = blk + jnp.where(sub_i >= d, pltpu.roll(blk, d, axis=0), 0.0)
                d *= 2
            if j % blk_per_seg != 0:
                blk = blk + carry
            carry = blk[SUBLANES - 1:SUBLANES, :]
            out.append(blk)
        return jnp.concatenate(out, axis=0)

    def load_state(s, hd):
        if hd < HEADS:
            return hgo_ref[s, hd]
        return jnp.concatenate([glo_ref[s, hd - HEADS], jnp.zeros((GLA_DK, HEAD_DIM), _F32)], axis=0)

    def store_state(s, hd, val):
        if hd < HEADS:
            hgo_ref[s, hd] = val
        else:
            glo_ref[s, hd - HEADS] = val[0:GLA_DK, :]

    def prepare(c, hd):
        src = pl.ds(c * chunk, chunk)
        sl = slice(hd * HEAD_DIM, (hd + 1) * HEAD_DIM)
        q = q_ref[src, sl]
        k = k_ref[src, sl]
        v = v_ref[src, sl]
        b = seg_cumsum(lf_ref[src, sl])
        level_ops = []
        q_bf = q.astype(_BF)
        k_bf = k.astype(_BF)
        for m, sgn in zip(levels, level_signs):
            b3 = b.reshape(chunk // (2 * m), 2 * m, HEAD_DIM)
            dec = jnp.exp2((b3 - b3[:, m - 1:m, :]) * sgn).reshape(chunk, HEAD_DIM).astype(_BF)
            level_ops.append((q_bf * dec, k_bf * dec))
        diag_cols = []
        for dd in range(DIAG):
            if dd == 0:
                diag_cols.append(jnp.sum(q * k, axis=-1, keepdims=True))
            else:
                e = jnp.where(row_diag >= dd, b - block_roll(b, dd), 0.0)
                diag_cols.append(
                    jnp.sum(q * block_roll(k, dd) * jnp.exp2(e), axis=-1, keepdims=True))
        b_last = b.reshape(nseg, seg, HEAD_DIM)[:, seg - 1:seg, :]
        q_in = q * jnp.exp2(b)
        k_dec = k * jnp.exp2((b_last - b.reshape(nseg, seg, HEAD_DIM)).reshape(chunk, HEAD_DIM))
        dec_cols = [jnp.sum(jnp.where(eye, jnp.exp2(b[(s + 1) * seg - 1:(s + 1) * seg, :]), 0.0),
                            axis=1, keepdims=True) for s in range(nseg)]
        return level_ops, diag_cols, q_in, k_dec, v, dec_cols

    def finish(c, hd, prepared):
        level_ops, diag_cols, q_in, k_dec, v, dec_cols = prepared
        r0 = c * chunk
        sl = slice(hd * HEAD_DIM, (hd + 1) * HEAD_DIM)
        att = jnp.zeros((chunk, chunk), _F32)
        for idx, (q_m, k_m) in enumerate(level_ops):
            a_m = lax.dot_general(q_m, k_m, (((1,), (1,)), ((), ())), preferred_element_type=_F32)
            att = jnp.where(src_ref[...] == 1 + idx, a_m, att)
        for dd in range(DIAG):
            att = jnp.where(src_ref[...] == 1 + len(levels) + dd, diag_cols[dd], att)
        att = att.astype(_BF)

        def state_update(sidx, state, dec_col, kd_s, v_s):
            upd = lax.dot_general(kd_s, v_s, (((0,), (0,)), ((), ())), preferred_element_type=_F32)
            store_state(sidx, hd, state * dec_col + upd)

        if nseg == 1:
            state = load_state(0, hd)
            v_bf = v.astype(_BF)
            lhs = jnp.concatenate([q_in.astype(_BF), att], axis=1)
            rhs = jnp.concatenate([state.astype(_BF), v_bf], axis=0)
            o_ref[pl.ds(r0, chunk), sl] = _dot(lhs, rhs)
            state_update(0, state, dec_cols[0], k_dec.astype(_BF), v_bf)
        else:
            o_intra = _dot(att, v.astype(_BF))
            for s in range(nseg):
                rs = slice(s * seg, (s + 1) * seg)
                state = load_state(s, hd)
                q_s, kd_s, v_s = q_in[rs], k_dec[rs], v[rs]
                if seg < BF16_ROWS:
                    zpad = jnp.zeros((BF16_ROWS - seg, HEAD_DIM), _F32)
                    q_s = jnp.concatenate([q_s, zpad], axis=0)
                    kd_s = jnp.concatenate([kd_s, zpad], axis=0)
                    v_s = jnp.concatenate([v_s, zpad], axis=0)
                o_inter = _dot(q_s.astype(_BF), state.astype(_BF))[0:seg, :]
                o_ref[pl.ds(r0 + s * seg, seg), sl] = o_intra[rs] + o_inter
                state_update(s, state, dec_cols[s], kd_s.astype(_BF), v_s.astype(_BF))

    units = [(c, hd) for c in range(nchunk) for hd in range(N_GEN_HEADS)]
    prepared = prepare(*units[0])
    for n, unit in enumerate(units):
        following = prepare(*units[n + 1]) if n + 1 < len(units) else None
        finish(*unit, prepared)
        prepared = following
        if slabs:
            slabs.pop(0)()
    while slabs:
        slabs.pop(0)()


    def normed(lo, norm_row):
        parts = []
        for hd in range(lo, lo + HEADS):
            o_h = o_ref[:, hd * HEAD_DIM:(hd + 1) * HEAD_DIM]
            parts.append(_rms_scale(o_h) * norm_row)
        return jnp.concatenate(parts, axis=1)

    y_c = normed(0, hgn_ref[...]) * g_ref[:, _G_CZ:_G_CZ + W_BR]
    y_d = normed(HEADS, gln_ref[...]) * g_ref[:, _G_DZ:_G_DZ + W_BR]
    merged = (mab_ref[...].reshape(rows, D_MODEL)
              + g_ref[:, _G_GC:_G_GC + D_MODEL] * _dot(y_c.astype(_BF), wbr_ref[0])
              + g_ref[:, _G_GD:_G_GD + D_MODEL] * _dot(y_d.astype(_BF), wbr_ref[1]))
    x_new = x_ref[...].reshape(rows, D_MODEL) + _dot(merged.astype(_BF), wout_ref[...])
    xo_ref[...] = x_new.reshape(tb, tl, D_MODEL)
    if last:
        yo_ref[...] = (_rms_scale(x_new) * fn_ref[...]).reshape(tb, tl, D_MODEL)


def _mix_cd(x, mab, st_hg, st_gla, hg_buf, gl_buf, p, layer, *, tb, tl, chunk, seg, last):
    bsz, seq, _ = x.shape
    rows = tb * tl
    grid = (bsz // tb, seq // tl)

    def lw(shape):
        nd = len(shape)
        return pl.BlockSpec((None,) + shape, lambda i, j, nd=nd: (layer,) + (0,) * nd,
                            pipeline_mode=pl.Buffered(1))

    act = pl.BlockSpec((tb, tl, D_MODEL), lambda i, j: (i, j, 0))
    hg_block = (None, tb, HEADS, HEAD_DIM, HEAD_DIM)
    gl_block = (None, tb, HEADS, GLA_DK, HEAD_DIM)

    def state_map(i, j):
        return (layer, i, 0, 0, 0)

    any_spec = pl.BlockSpec(memory_space=pl.ANY)
    in_specs = [
        act, act,
        pl.BlockSpec(hg_block, state_map), pl.BlockSpec(gl_block, state_map),
        lw((1, D_MODEL)), lw((D_MODEL, _B_COLS)), lw((1, W_BR)), lw((1, W_BR)), lw((1, W_BR)),
        lw((1, HEAD_DIM)), lw((LANES, W_BR)), lw((1, W_BR)),
        pl.BlockSpec((1, W_BR), lambda i, j: (0, 0)),
        lw((1, HEAD_DIM)), lw((2, W_BR, D_MODEL)), lw((D_MODEL, D_MODEL)),
        pl.BlockSpec((1, D_MODEL), lambda i, j: (0, 0)),
        pl.BlockSpec((chunk, chunk), lambda i, j: (0, 0), pipeline_mode=pl.Buffered(1)),
        any_spec, any_spec,
    ]
    n_in = len(in_specs)
    act_shape = jax.ShapeDtypeStruct((bsz, seq, D_MODEL), _F32)
    n_act_out = 2 if last else 1
    out_specs = [act] * n_act_out + [pl.BlockSpec(hg_block, state_map),
                                     pl.BlockSpec(gl_block, state_map)]
    out_shape = [act_shape] * n_act_out + [
        jax.ShapeDtypeStruct(hg_buf.shape, _F32), jax.ShapeDtypeStruct(gl_buf.shape, _F32)]
    wide = N_GEN_HEADS * HEAD_DIM
    scratch = [pltpu.VMEM((rows, wide), _F32) for _ in range(4)] + [
        pltpu.VMEM((rows, _G_COLS), _F32),
        pltpu.VMEM((rows, wide), _F32),
        pltpu.VMEM((rows, D_MODEL), _BF)]
    return pl.pallas_call(
        functools.partial(_mix_cd_kernel, tb=tb, tl=tl, chunk=chunk, seg=seg, last=last),
        grid=grid, in_specs=in_specs, out_specs=out_specs, out_shape=out_shape,
        scratch_shapes=scratch, name="mix_cd",
        input_output_aliases={n_in - 2: n_act_out, n_in - 1: n_act_out + 1},
        compiler_params=pltpu.CompilerParams(
            dimension_semantics=("arbitrary", "arbitrary"), vmem_limit_bytes=VMEM_LIMIT_BYTES),
    )(x, mab, st_hg, st_gla, p["norm_w"], p["w_b"], p["log_lb"], p["log1m_lb"], p["onem_lb"],
      p["hgrn_norm"], p["w_gk"], p["b_gk"], p["gk_scale"], p["gla_norm"], p["w_br_cd"], p["w_out"],
      p["final_norm"], jnp.asarray(_att_source(chunk, seg)), hg_buf, gl_buf)


def _pad_gla_heads(w):
    lead = w.shape[:-1]
    w4 = w.reshape(lead + (HEADS, GLA_DK))
    w4 = jnp.pad(w4, [(0, 0)] * len(lead) + [(0, 0), (0, HEAD_DIM - GLA_DK)])
    return w4.reshape(lead + (HEADS * HEAD_DIM,))


def _pack_w_in_kernel(w_ref, wa_ref, wb_ref):
    rows = w_ref.shape[0]

    def cols(lo, n):
        return w_ref[:, lo:lo + n]

    def padded_heads(lo):
        zeros = jnp.zeros((rows, HEAD_DIM - GLA_DK), _F32)
        parts = []
        for hd in range(HEADS):
            parts += [cols(lo + hd * GLA_DK, GLA_DK), zeros]
        return jnp.concatenate(parts, axis=1)

    wa_ref[:, 0:_OFF_CQ] = cols(0, _OFF_CQ).astype(_BF)
    wa_ref[:, _A_GA:_A_COLS] = cols(_OFF_GATE, 2 * D_MODEL).astype(_BF)
    wb_ref[:, _B_CQ:_B_DQ] = cols(_OFF_CQ, _OFF_DQ - _OFF_CQ).astype(_BF)
    wb_ref[:, _B_DQ:_B_DK] = padded_heads(_OFF_DQ).astype(_BF)
    wb_ref[:, _B_DK:_B_DV] = padded_heads(_OFF_DK).astype(_BF)
    wb_ref[:, _B_DV:_B_DR] = cols(_OFF_DV, _OFF_DR - _OFF_DV).astype(_BF)
    wb_ref[:, _B_DR:_B_GC] = jnp.concatenate(
        [cols(_OFF_DR, GLA_RANK), jnp.zeros((rows, LANES - GLA_RANK), _F32)], axis=1).astype(_BF)
    wb_ref[:, _B_GC:_B_COLS] = cols(_OFF_GATE + 2 * D_MODEL, 2 * D_MODEL).astype(_BF)


def _pack_w_in(w_in):
    d_in = w_in.shape[-1]
    return pl.pallas_call(
        _pack_w_in_kernel,
        grid=(DEPTH, D_MODEL // _PACK_ROWS),
        in_specs=[pl.BlockSpec((None, _PACK_ROWS, d_in), lambda l, r: (l, r, 0))],
        out_specs=[pl.BlockSpec((None, _PACK_ROWS, _A_COLS), lambda l, r: (l, r, 0)),
                   pl.BlockSpec((None, _PACK_ROWS, _B_COLS), lambda l, r: (l, r, 0))],
        out_shape=[jax.ShapeDtypeStruct((DEPTH, D_MODEL, _A_COLS), _BF),
                   jax.ShapeDtypeStruct((DEPTH, D_MODEL, _B_COLS), _BF)],
        name="pack_w_in",
        compiler_params=pltpu.CompilerParams(
            dimension_semantics=("arbitrary", "arbitrary"), vmem_limit_bytes=VMEM_LIMIT_BYTES),
    )(w_in)


def _prep_params(norm_w, w_in, conv_w, s5_a_re, s5_a_im, s5_log_dt, s5_b_re, s5_b_im, s5_c_re,
                 s5_c_im, s5_d, w_glu, b_glu, hgrn_lb_raw, hgrn_norm, w_gk, b_gk, gla_norm,
                 w_branch, w_out, final_norm):
    f32 = _F32
    p = {}
    p["norm_w"] = norm_w.astype(f32)[:, None, :]
    p["w_a"], p["w_b"] = _pack_w_in(w_in)
    p["conv_w"] = conv_w.astype(f32)

    ar = s5_a_re.astype(f32)
    ai = s5_a_im.astype(f32)
    dt = jnp.exp(s5_log_dt.astype(f32))[..., None]
    mag = jnp.exp(dt * ar)
    abar_r = mag * jnp.cos(dt * ai)
    abar_i = mag * jnp.sin(dt * ai)
    den = ar * ar + ai * ai
    zr = ((abar_r - 1.0) * ar + abar_i * ai) / den
    zi = (abar_i * ar - (abar_r - 1.0) * ai) / den
    b_re = s5_b_re.astype(f32)
    b_im = s5_b_im.astype(f32)
    bbar_r = zr[..., None] * b_re - zi[..., None] * b_im
    bbar_i = zr[..., None] * b_im + zi[..., None] * b_re
    eye = jnp.eye(S5_GROUP, dtype=f32)

    def bd_in(bb):
        x = bb.reshape(DEPTH, 2, S5_GROUP, S5_STATE, S5_GROUP)
        x = jnp.transpose(x, (0, 1, 2, 4, 3))[:, :, :, :, None, :]
        x = x * eye[None, None, :, None, :, None]
        return x.reshape(DEPTH, 2, 256, 1024)

    def bd_out(c):
        x = c.reshape(DEPTH, 2, S5_GROUP, S5_GROUP, S5_STATE)
        x = jnp.transpose(x, (0, 1, 2, 4, 3))[:, :, :, :, None, :]
        x = x * eye[None, None, :, None, :, None]
        return x.reshape(DEPTH, 2, 1024, 256)

    p["bdb"] = jnp.stack([bd_in(bbar_r), bd_in(bbar_i)], axis=2).astype(_BF)
    p["bdc"] = jnp.stack([bd_out(s5_c_re.astype(f32)), bd_out(-s5_c_im.astype(f32))],
                         axis=2).astype(_BF)
    p["a_r"] = abar_r.reshape(DEPTH, 1, S5_FLAT)
    p["a_i"] = abar_i.reshape(DEPTH, 1, S5_FLAT)
    p["s5_d"] = s5_d.astype(f32)[:, None, :]
    p["w_glu"] = w_glu.astype(_BF)
    p["b_glu"] = b_glu.astype(f32)[:, None, :]
    p["w_br_ab"] = w_branch[:, 0:2].astype(_BF)
    p["w_br_cd"] = w_branch[:, 2:4].astype(_BF)
    p["w_out"] = w_out.astype(_BF)

    lb_cum = jnp.cumsum(jax.nn.softmax(hgrn_lb_raw.astype(f32), axis=0), axis=0)
    lb = lb_cum - lb_cum[0:1]
    p["log_lb"] = jnp.log(lb)[:, None, :]
    p["log1m_lb"] = jnp.log1p(-lb)[:, None, :]
    p["onem_lb"] = (1.0 - lb)[:, None, :]
    p["hgrn_norm"] = hgrn_norm.astype(f32)[:, None, :]
    p["gla_norm"] = gla_norm.astype(f32)[:, None, :]
    p["w_gk"] = jnp.pad(_pad_gla_heads(w_gk), ((0, 0), (0, LANES - GLA_RANK), (0, 0))).astype(_BF)
    p["b_gk"] = _pad_gla_heads(b_gk.astype(f32))[:, None, :]
    p["gk_scale"] = _pad_gla_heads(jnp.full((1, HEADS * GLA_DK), _LOG2E / GLA_GATE_NORM, f32))
    p["final_norm"] = final_norm.astype(f32)[None, :]
    return p


def _trunk(x, st_conv, st_re, st_im, st_hg, st_gla, p, *, ab_cfg, cd_cfg):
    bsz = x.shape[0]
    new_conv, new_re, new_im = [], [], []
    y = None
    st_re = st_re.reshape(DEPTH, bsz, S5_FLAT)
    st_im = st_im.reshape(DEPTH, bsz, S5_FLAT)
    new_hg = jnp.zeros(st_hg.shape, _F32)
    new_gla = jnp.zeros(st_gla.shape, _F32)
    for layer in range(DEPTH):
        mab, cbuf, h_r, h_i = _mix_ab(x, st_conv, st_re, st_im, p, layer, **ab_cfg)
        last = layer == DEPTH - 1
        outs = _mix_cd(x, mab, st_hg, st_gla, new_hg, new_gla, p, layer, last=last, **cd_cfg)
        if last:
            x, y, new_hg, new_gla = outs
        else:
            x, new_hg, new_gla = outs
        new_conv.append(cbuf)
        new_re.append(h_r.reshape(bsz, S5_GROUPS, S5_STATE))
        new_im.append(h_i.reshape(bsz, S5_GROUPS, S5_STATE))
    return (y, jnp.stack(new_conv, 0), jnp.stack(new_re, 0), jnp.stack(new_im, 0), new_hg, new_gla)


_PROMPT_AB = dict(tb=8, tl=64)
_PROMPT_CD = dict(tb=1, tl=256, chunk=128, seg=128)
_SAMPLE_AB = dict(tb=32, tl=8)
_SAMPLE_CD = dict(tb=16, tl=8, chunk=128, seg=8)


@jax.jit
def _forward(x_prompt, x_sample, state_conv, state_ssm_re, state_ssm_im, state_hgrn, state_gla,
             *params):
    p = _prep_params(*params)
    bp = x_prompt.shape[0]
    dt = x_prompt.dtype
    z_conv = jnp.zeros((DEPTH, bp, 2, W_BR), dt)
    z_ssm = jnp.zeros((DEPTH, bp, S5_GROUPS, S5_STATE), dt)
    z_hg = jnp.zeros((DEPTH, bp, HEADS, HEAD_DIM, HEAD_DIM), dt)
    z_gla = jnp.zeros((DEPTH, bp, HEADS, GLA_DK, HEAD_DIM), dt)
    y_p, conv_p, re_p, im_p, hg_p, gla_p = _trunk(
        x_prompt, z_conv, z_ssm, z_ssm, z_hg, z_gla, p, ab_cfg=_PROMPT_AB, cd_cfg=_PROMPT_CD)
    y_s, conv_s, re_s, im_s, hg_s, gla_s = _trunk(
        x_sample, state_conv, state_ssm_re, state_ssm_im, state_hgrn, state_gla, p,
        ab_cfg=_SAMPLE_AB, cd_cfg=_SAMPLE_CD)
    return (y_p, y_s, conv_p, conv_s, re_p, re_s, im_p, im_s, hg_p, hg_s, gla_p, gla_s)


def kernel(x_prompt, x_sample, state_conv, state_ssm_re, state_ssm_im, state_hgrn, state_gla,
           norm_w, w_in, conv_w, s5_a_re, s5_a_im, s5_log_dt, s5_b_re, s5_b_im, s5_c_re, s5_c_im,
           s5_d, w_glu, b_glu, hgrn_lb_raw, hgrn_norm, w_gk, b_gk, gla_norm, w_branch, w_out,
           final_norm):
    return _forward(x_prompt, x_sample, state_conv, state_ssm_re, state_ssm_im, state_hgrn,
                    state_gla, norm_w, w_in, conv_w, s5_a_re, s5_a_im, s5_log_dt, s5_b_re, s5_b_im,
                    s5_c_re, s5_c_im, s5_d, w_glu, b_glu, hgrn_lb_raw, hgrn_norm, w_gk, b_gk,
                    gla_norm, w_branch, w_out, final_norm)
```
